```python
import math
import jax, jax.numpy as jnp
from jax import lax
import numpy as np


D_MODEL = 1024
BATCH = 4
SEQ = 4096
DEPTH = 4

MEM_LEN = 256
D_FF = 2816
D_CONV = 512
CONV_WIDTH = 31
D_SGU = 512
SGU_GROUPS = 4
SGU_CHUNK = 128
SGU_GROUP_DIM = D_SGU // SGU_GROUPS
NSA_HEADS = 8
NSA_KV_GROUPS = 2
NSA_HEADS_PER_GROUP = NSA_HEADS // NSA_KV_GROUPS
NSA_HEAD_DIM = 64
CMP_LEN = 32
CMP_STRIDE = 16
SEL_BLOCK = 64
TOP_N = 16
WINDOW = 512
Q_BLOCK = 128
X_HEADS = 4
X_HEAD_DIM = 128
N_BRANCHES = 3
NEG_INF = -1e30
FORCE_SCORE = 1e4
EPS = 1e-6

COLS_CONV = 2 * D_CONV
COLS_SGU = 2 * D_SGU
COLS_Q = NSA_HEADS * NSA_HEAD_DIM
COLS_KV = 3 * 2 * NSA_KV_GROUPS * NSA_HEAD_DIM
COLS_NSA_GATE = NSA_HEADS * 3
COLS_MERGE = N_BRANCHES * D_MODEL
IN_COLS = COLS_CONV + COLS_SGU + COLS_Q + COLS_KV + COLS_NSA_GATE + COLS_MERGE
SPLITS = [COLS_CONV,
          COLS_CONV + COLS_SGU,
          COLS_CONV + COLS_SGU + COLS_Q,
          COLS_CONV + COLS_SGU + COLS_Q + COLS_KV,
          COLS_CONV + COLS_SGU + COLS_Q + COLS_KV + COLS_NSA_GATE]

kernel_name = 'hybrid_conv_sgu_nsa_macaron_block'


def _rms_norm(x, g):
    xf = x.astype(jnp.float32)
    y = xf * lax.rsqrt(jnp.mean(xf * xf, axis=-1, keepdims=True) + EPS)
    return (y * g.astype(jnp.float32)).astype(x.dtype)


def _layer_norm(x, g, b):
    xf = x.astype(jnp.float32)
    mu = jnp.mean(xf, axis=-1, keepdims=True)
    var = jnp.mean(jnp.square(xf - mu), axis=-1, keepdims=True)
    y = (xf - mu) * lax.rsqrt(var + EPS)
    return (y * g.astype(jnp.float32) + b.astype(jnp.float32)).astype(x.dtype)


def _masked_softmax(s, mask):
    return jax.nn.softmax(jnp.where(mask, s, NEG_INF), axis=-1)


def _alibi_slopes(n):
    return jnp.asarray(2.0 ** (-8.0 * np.arange(1, n + 1) / n), dtype=jnp.float32)


def _swiglu(x, wi, wo):
    a, b = jnp.split(x @ wi, 2, axis=-1)
    return (jax.nn.silu(a) * b) @ wo


def _conv_module(z, w, bias, ln_g, ln_b):
    a, g = jnp.split(z, 2, axis=-1)
    h = a * jax.nn.sigmoid(g)
    h = lax.conv_general_dilated(h, w[:, None, :], window_strides=(1,),
                                 padding=[(CONV_WIDTH - 1, 0)],
                                 dimension_numbers=('NWC', 'WIO', 'NWC'),
                                 feature_group_count=D_CONV) + bias
    return jax.nn.silu(_layer_norm(h, ln_g, ln_b))


def _spatial_gating(z, norm_g, ws, b):
    B, T = z.shape[0], z.shape[1]
    u, v = jnp.split(jax.nn.gelu(z), 2, axis=-1)
    v = _rms_norm(v, norm_g).reshape(B, T // SGU_CHUNK, SGU_CHUNK, SGU_GROUPS, SGU_GROUP_DIM)
    mixed = jnp.einsum('gts,bcsgk->bctgk', jnp.tril(ws), v) + b.T[None, None, :, :, None]
    return u * mixed.reshape(B, T, D_SGU)


def _compress(x, idx, pe, w1, w2):
    B, G = x.shape[0], x.shape[2]
    blocks = x[:, idx] + pe[:, None, :]
    blocks = jnp.swapaxes(blocks, 2, 3).reshape(B, idx.shape[0], G, -1)
    return jax.nn.silu(blocks @ w1) @ w2


def _nsa(q, kv, gate_logits, q_g, k_g, cmp_pe, cmp_w1, cmp_w2):
    B, T = q.shape[0], q.shape[1]
    G, HG, dh = NSA_KV_GROUPS, NSA_HEADS_PER_GROUP, NSA_HEAD_DIM
    scale = 1.0 / math.sqrt(dh)
    slopes = _alibi_slopes(NSA_HEADS).reshape(G, HG)
    q = _rms_norm(q.reshape(B, T, G, HG, dh), q_g)
    kv = kv.reshape(B, T, 6, G, dh)
    k_cmp, v_cmp, k_sel, v_sel, k_win, v_win = (kv[:, :, i] for i in range(6))
    k_sel = _rms_norm(k_sel, k_g[1])
    k_win = _rms_norm(k_win, k_g[2])

    n_cmp = (T - CMP_LEN) // CMP_STRIDE + 1
    cmp_start = CMP_STRIDE * jnp.arange(n_cmp)
    cmp_idx = cmp_start[:, None] + jnp.arange(CMP_LEN)[None, :]
    cmp_end = cmp_start + CMP_LEN - 1
    kc = _rms_norm(_compress(k_cmp, cmp_idx, cmp_pe[0], cmp_w1[0], cmp_w2[0]), k_g[0])
    vc = _compress(v_cmp, cmp_idx, cmp_pe[1], cmp_w1[1], cmp_w2[1])

    n_sel = T // SEL_BLOCK
    n_top = min(TOP_N, n_sel)
    sel_start = SEL_BLOCK * jnp.arange(n_sel)
    overlap = ((cmp_start[:, None] < sel_start[None, :] + SEL_BLOCK)
               & (cmp_end[:, None] >= sel_start[None, :])).astype(jnp.float32)
    kb = k_sel.reshape(B, n_sel, SEL_BLOCK, G, dh).transpose(0, 3, 1, 2, 4)
    vb = v_sel.reshape(B, n_sel, SEL_BLOCK, G, dh).transpose(0, 3, 1, 2, 4)

    kp = jnp.pad(k_win, ((0, 0), (WINDOW, 0), (0, 0), (0, 0)))
    vp = jnp.pad(v_win, ((0, 0), (WINDOW, 0), (0, 0), (0, 0)))

    n_qb = T // Q_BLOCK
    qb = q.reshape(B, n_qb, Q_BLOCK, G, HG, dh).transpose(1, 0, 2, 3, 4, 5)
    gb = jax.nn.sigmoid(gate_logits).reshape(B, n_qb, Q_BLOCK, G, HG, 3).transpose(1, 0, 2, 3, 4, 5)
    gather = jax.vmap(jax.vmap(lambda blocks, ix: blocks[ix]))

    def block_fn(args):
        c, qc, gc = args
        t = c * Q_BLOCK + jnp.arange(Q_BLOCK)
        d_c = t[:, None] - cmp_end[None, :]
        m_c = d_c >= 0
        s_c = (jnp.einsum('bqghd,bngd->bghqn', qc, kc).astype(jnp.float32) * scale
               - slopes[:, :, None, None] * d_c.astype(jnp.float32))
        p_c = _masked_softmax(s_c, m_c) * m_c
        o_c = jnp.einsum('bghqn,bngd->bqghd', p_c.astype(vc.dtype), vc)
        imp = jnp.einsum('bghqn,nj->bgqj', p_c, overlap)
        cur = t // SEL_BLOCK
        j = jnp.arange(n_sel)
        forced = (j[None, :] == 0) | (j[None, :] == cur[:, None]) | (j[None, :] == cur[:, None] - 1)
        imp = jnp.where(sel_start[None, :] <= t[:, None], jnp.where(forced, FORCE_SCORE, imp), NEG_INF)
        top_val, top_idx = lax.top_k(imp, n_top)
        ks = gather(kb, top_idx)
        vs = gather(vb, top_idx)
        pos = top_idx[..., None] * SEL_BLOCK + jnp.arange(SEL_BLOCK)
        d_s = t[None, None, :, None, None] - pos
        m_s = (top_val > 0.5 * NEG_INF)[..., None] & (d_s >= 0)
        s_s = (jnp.einsum('bqghd,bgqnkd->bghqnk', qc, ks).astype(jnp.float32) * scale
               - slopes[:, :, None, None, None] * d_s[:, :, None].astype(jnp.float32))
        s_s = s_s.reshape(B, G, HG, Q_BLOCK, n_top * SEL_BLOCK)
        m_s = m_s.reshape(B, G, 1, Q_BLOCK, n_top * SEL_BLOCK)
        p_s = _masked_softmax(s_s, m_s)
        o_s = jnp.einsum('bghqm,bgqmd->bqghd', p_s.astype(vs.dtype),
                         vs.reshape(B, G, Q_BLOCK, n_top * SEL_BLOCK, dh))
        kw = lax.dynamic_slice_in_dim(kp, c * Q_BLOCK, Q_BLOCK + WINDOW, axis=1)
        vw = lax.dynamic_slice_in_dim(vp, c * Q_BLOCK, Q_BLOCK + WINDOW, axis=1)
        s_pos = c * Q_BLOCK - WINDOW + jnp.arange(Q_BLOCK + WINDOW)
        d_w = t[:, None] - s_pos[None, :]
        m_w = (d_w >= 0) & (d_w < WINDOW) & (s_pos[None, :] >= 0)
        s_w = (jnp.einsum('bqghd,bsgd->bghqs', qc, kw).astype(jnp.float32) * scale
               - slopes[:, :, None, None] * d_w.astype(jnp.float32))
        p_w = _masked_softmax(s_w, m_w)
        o_w = jnp.einsum('bghqs,bsgd->bqghd', p_w.astype(vw.dtype), vw)
        return gc[..., 0:1] * o_c + gc[..., 1:2] * o_s + gc[..., 2:3] * o_w

    out = lax.map(block_fn, (jnp.arange(n_qb), qb, gb))
    return out.transpose(1, 0, 2, 3, 4, 5).reshape(B, T, NSA_HEADS * dh)


def _memory_cross_attention(h, mem_n, wq, wkv, q_g, k_g, wo):
    B, T = h.shape[0], h.shape[1]
    M = mem_n.shape[1]
    q = _rms_norm((h @ wq).reshape(B, T, X_HEADS, X_HEAD_DIM), q_g)
    k, v = jnp.split(mem_n @ wkv, 2, axis=-1)
    k = _rms_norm(k.reshape(B, M, X_HEADS, X_HEAD_DIM), k_g)
    v = v.reshape(B, M, X_HEADS, X_HEAD_DIM)
    s = jnp.einsum('bthd,bmhd->bhtm', q, k).astype(jnp.float32) / math.sqrt(X_HEAD_DIM)
    p = jax.nn.softmax(s, axis=-1)
    o = jnp.einsum('bhtm,bmhd->bthd', p.astype(v.dtype), v)
    return o.reshape(B, T, X_HEADS * X_HEAD_DIM) @ wo


def setup_inputs(seed: int = 0) -> dict:
    key = jax.random.key(seed)
    ks = list(jax.random.split(key, 40))
    L = DEPTH
    dh = NSA_HEAD_DIM

    def nrm(shape, scale):
        return jax.random.normal(ks.pop(), shape, jnp.float32) * scale

    def gain(shape):
        return 1.0 + nrm(shape, 0.02)

    return {
        'x': nrm((BATCH, SEQ, D_MODEL), 1.0),
        'mem': nrm((BATCH, MEM_LEN, D_MODEL), 1.0),
        'norm_ffn1': gain((L, D_MODEL)),
        'ffn1_wi': nrm((L, D_MODEL, 2 * D_FF), D_MODEL ** -0.5),
        'ffn1_wo': nrm((L, D_FF, D_MODEL), D_FF ** -0.5),
        'norm_mix': gain((L, D_MODEL)),
        'w_in': nrm((L, D_MODEL, IN_COLS), D_MODEL ** -0.5),
        'conv_w': nrm((L, CONV_WIDTH, D_CONV), CONV_WIDTH ** -0.5),
        'conv_b': nrm((L, D_CONV), 0.02),
        'conv_ln_g': gain((L, D_CONV)),
        'conv_ln_b': nrm((L, D_CONV), 0.02),
        'conv_out': nrm((L, D_CONV, D_MODEL), D_CONV ** -0.5),
        'sgu_norm': gain((L, D_SGU)),
        'sgu_ws': nrm((L, SGU_GROUPS, SGU_CHUNK, SGU_CHUNK), SGU_CHUNK ** -0.5),
        'sgu_b': nrm((L, SGU_GROUPS, SGU_CHUNK), 0.02),
        'sgu_out': nrm((L, D_SGU, D_MODEL), D_SGU ** -0.5),
        'nsa_q_norm': gain((L, dh)),
        'nsa_k_norm': gain((L, 3, dh)),
        'cmp_pe': nrm((L, 2, CMP_LEN, dh), 0.1),
        'cmp_w1': nrm((L, 2, CMP_LEN * dh, dh), (CMP_LEN * dh) ** -0.5),
        'cmp_w2': nrm((L, 2, dh, dh), dh ** -0.5),
        'nsa_out': nrm((L, NSA_HEADS * dh, D_MODEL), (NSA_HEADS * dh) ** -0.5),
        'w_out': nrm((L, D_MODEL, D_MODEL), D_MODEL ** -0.5),
        'norm_xattn': gain((L, D_MODEL)),
        'mem_norm': gain((L, D_MODEL)),
        'xq': nrm((L, D_MODEL, X_HEADS * X_HEAD_DIM), D_MODEL ** -0.5),
        'xkv': nrm((L, D_MODEL, 2 * X_HEADS * X_HEAD_DIM), D_MODEL ** -0.5),
        'xq_norm': gain((L, X_HEAD_DIM)),
        'xk_norm': gain((L, X_HEAD_DIM)),
        'xo': nrm((L, X_HEADS * X_HEAD_DIM, D_MODEL), (X_HEADS * X_HEAD_DIM) ** -0.5),
        'norm_ffn2': gain((L, D_MODEL)),
        'ffn2_wi': nrm((L, D_MODEL, 2 * D_FF), D_MODEL ** -0.5),
        'ffn2_wo': nrm((L, D_FF, D_MODEL), D_FF ** -0.5),
    }


def reference(x, mem, norm_ffn1, ffn1_wi, ffn1_wo, norm_mix, w_in,
              conv_w, conv_b, conv_ln_g, conv_ln_b, conv_out,
              sgu_norm, sgu_ws, sgu_b, sgu_out,
              nsa_q_norm, nsa_k_norm, cmp_pe, cmp_w1, cmp_w2, nsa_out,
              w_out, norm_xattn, mem_norm, xq, xkv, xq_norm, xk_norm, xo,
              norm_ffn2, ffn2_wi, ffn2_wo):
    h = x
    for l in range(DEPTH):
        h = h + 0.5 * _swiglu(_rms_norm(h, norm_ffn1[l]), ffn1_wi[l], ffn1_wo[l])
        n = _rms_norm(h, norm_mix[l])
        z = n @ w_in[l]
        z_conv, z_sgu, z_q, z_kv, z_ngate, z_merge = jnp.split(z, SPLITS, axis=-1)
        y_a = _conv_module(z_conv, conv_w[l], conv_b[l], conv_ln_g[l], conv_ln_b[l]) @ conv_out[l]
        y_b = _spatial_gating(z_sgu, sgu_norm[l], sgu_ws[l], sgu_b[l]) @ sgu_out[l]
        y_c = _nsa(z_q, z_kv, z_ngate, nsa_q_norm[l], nsa_k_norm[l],
                   cmp_pe[l], cmp_w1[l], cmp_w2[l]) @ nsa_out[l]
        g_a, g_b, g_c = jnp.split(jax.nn.sigmoid(z_merge), N_BRANCHES, axis=-1)
        h = h + (g_a * y_a + g_b * y_b + g_c * y_c) @ w_out[l]
        h = h + _memory_cross_attention(_rms_norm(h, norm_xattn[l]), _rms_norm(mem, mem_norm[l]),
                                        xq[l], xkv[l], xq_norm[l], xk_norm[l], xo[l])
        h = h + 0.5 * _swiglu(_rms_norm(h, norm_ffn2[l]), ffn2_wi[l], ffn2_wo[l])
    return h
```

```python
import functools
import math

import numpy as np
import jax
import jax.numpy as jnp
from jax import lax
from jax.experimental import pallas as pl
from jax.experimental.pallas import tpu as pltpu

F32 = jnp.float32
BF16 = jnp.bfloat16

EPS = 1e-6
NEG_INF = -1e30
FORCE_SCORE = 1e4

CONV_WIDTH = 31
CONV_HALO = 32
SGU_CHUNK = 128
SGU_GROUPS = 4
NSA_HEADS = 8
NSA_KV_GROUPS = 2
NSA_HPG = NSA_HEADS // NSA_KV_GROUPS
NSA_DH = 64
CMP_LEN = 32
CMP_STRIDE = 16
SEL_BLOCK = 64
TOP_N = 16
WINDOW = 512
Q_BLOCK = 128
X_HEADS = 4
X_DH = 128

ROW_TILE = 512
INPROJ_ROW_TILE = 256
SEL_KEY_CHUNK = 256
VMEM_LIMIT = 56 * 1024 * 1024


def _cparams(*sem):
    return pltpu.CompilerParams(dimension_semantics=sem, vmem_limit_bytes=VMEM_LIMIT)


def _resident(shape):
    nd = len(shape)
    return pl.BlockSpec(shape, lambda *_: (0,) * nd, pipeline_mode=pl.Buffered(1))


def _rms(x, g):
    return x * lax.rsqrt(jnp.mean(x * x, axis=-1, keepdims=True) + EPS) * g


def _dot(a, b):
    return jnp.dot(a, b, preferred_element_type=F32)


def _dot_nt(a, b):
    return lax.dot_general(a, b, (((1,), (1,)), ((), ())), preferred_element_type=F32)


def _ffn_kernel(x_ref, g_ref, wi_ref, wo_ref, o_ref, *, d_ff, n_chunks):
    x = x_ref[...]
    xn = _rms(x, g_ref[...]).astype(BF16)
    ck = d_ff // n_chunks
    acc = x
    for c in range(n_chunks):
        a = _dot(xn, wi_ref[:, c * ck:(c + 1) * ck])
        b = _dot(xn, wi_ref[:, d_ff + c * ck:d_ff + (c + 1) * ck])
        mid = (a * jax.nn.sigmoid(a) * b).astype(BF16)
        acc = acc + 0.5 * _dot(mid, wo_ref[c * ck:(c + 1) * ck, :])
    o_ref[...] = acc


def _ffn(h, g, wi, wo):
    m, d = h.shape
    d_ff = wo.shape[0]
    tm = ROW_TILE
    return pl.pallas_call(
        functools.partial(_ffn_kernel, d_ff=d_ff, n_chunks=2),
        grid=(m // tm,),
        in_specs=[pl.BlockSpec((tm, d), lambda i: (i, 0)),
                  _resident((1, d)), _resident(wi.shape), _resident(wo.shape)],
        out_specs=pl.BlockSpec((tm, d), lambda i: (i, 0)),
        out_shape=jax.ShapeDtypeStruct((m, d), F32),
        compiler_params=_cparams("parallel"),
        name="ffn",
    )(h, g.reshape(1, d), wi, wo)


def _inproj_kernel(x_ref, g_ref, w_ref, *o_refs):
    xn = _rms(x_ref[...], g_ref[...]).astype(BF16)
    off = 0
    for o_ref in o_refs:
        n = o_ref.shape[1]
        o_ref[...] = _dot(xn, w_ref[:, off:off + n]).astype(o_ref.dtype)
        off += n


def _inproj(h, g, w, widths):
    m, d = h.shape
    tm = INPROJ_ROW_TILE
    return pl.pallas_call(
        _inproj_kernel,
        grid=(m // tm,),
        in_specs=[pl.BlockSpec((tm, d), lambda i: (i, 0)), _resident((1, d)), _resident(w.shape)],
        out_specs=[pl.BlockSpec((tm, n), lambda i: (i, 0)) for n in widths],
        out_shape=[jax.ShapeDtypeStruct((m, n), F32) for n in widths],
        compiler_params=_cparams("parallel"),
        name="inproj",
    )(h, g.reshape(1, d), w)


def _conv_kernel(cur_ref, prev_ref, w_ref, b_ref, lg_ref, lb_ref, o_ref, buf_ref, *, d_conv):
    i = pl.program_id(1)
    tt = cur_ref.shape[0]
    cur = cur_ref[...]
    prev = prev_ref[...]
    keep = (i > 0).astype(F32)
    buf_ref[0:CONV_HALO, :] = prev[:, :d_conv] * jax.nn.sigmoid(prev[:, d_conv:]) * keep
    buf_ref[CONV_HALO:, :] = cur[:, :d_conv] * jax.nn.sigmoid(cur[:, d_conv:])
    base = CONV_HALO - (CONV_WIDTH - 1)
    acc = jnp.zeros((tt, d_conv), F32)
    for k in range(CONV_WIDTH):
        acc = acc + w_ref[k:k + 1, :] * buf_ref[base + k:base + k + tt, :]
    y = acc + b_ref[...]
    mu = jnp.mean(y, axis=-1, keepdims=True)
    yc = y - mu
    var = jnp.mean(yc * yc, axis=-1, keepdims=True)
    yn = yc * lax.rsqrt(var + EPS) * lg_ref[...] + lb_ref[...]
    o_ref[...] = (yn * jax.nn.sigmoid(yn)).astype(o_ref.dtype)


def _conv(z, w, b, lg, lb, batch):
    m, two_d = z.shape
    d_conv = two_d // 2
    t = m // batch
    tt = ROW_TILE
    per_b = t // tt
    halo_per_tile = tt // CONV_HALO
    return pl.pallas_call(
        functools.partial(_conv_kernel, d_conv=d_conv),
        grid=(batch, per_b),
        in_specs=[pl.BlockSpec((tt, two_d), lambda bb, i: (bb * per_b + i, 0)),
                  pl.BlockSpec((CONV_HALO, two_d),
                               lambda bb, i: (jnp.maximum((bb * per_b + i) * halo_per_tile - 1, 0), 0)),
                  _resident(w.shape), _resident((1, d_conv)), _resident((1, d_conv)), _resident((1, d_conv))],
        out_specs=pl.BlockSpec((tt, d_conv), lambda bb, i: (bb * per_b + i, 0)),
        out_shape=jax.ShapeDtypeStruct((m, d_conv), BF16),
        scratch_shapes=[pltpu.VMEM((CONV_HALO + tt, d_conv), F32)],
        compiler_params=_cparams("parallel", "parallel"),
        name="conv",
    )(z, z, w, b.reshape(1, -1), lg.reshape(1, -1), lb.reshape(1, -1))


def _sgu_kernel(z_ref, g_ref, ws_ref, b_ref, o_ref, *, d_sgu):
    tm = z_ref.shape[0]
    gd = d_sgu // SGU_GROUPS
    ge = jax.nn.gelu(z_ref[...])
    u = ge[:, :d_sgu]
    vn = _rms(ge[:, d_sgu:], g_ref[...]).astype(BF16)
    row = lax.broadcasted_iota(jnp.int32, (SGU_CHUNK, SGU_CHUNK), 0)
    col = lax.broadcasted_iota(jnp.int32, (SGU_CHUNK, SGU_CHUNK), 1)
    for gr in range(SGU_GROUPS):
        wt = jnp.where(row >= col, ws_ref[gr], 0.0).astype(BF16)
        for ch in range(tm // SGU_CHUNK):
            rows = slice(ch * SGU_CHUNK, (ch + 1) * SGU_CHUNK)
            cols = slice(gr * gd, (gr + 1) * gd)
            mixed = _dot(wt, vn[rows, cols]) + b_ref[:, cols]
            o_ref[rows, cols] = (u[rows, cols] * mixed).astype(o_ref.dtype)


def _sgu(z, g, ws, b_exp):
    m, two_d = z.shape
    d_sgu = two_d // 2
    tm = ROW_TILE
    return pl.pallas_call(
        functools.partial(_sgu_kernel, d_sgu=d_sgu),
        grid=(m // tm,),
        in_specs=[pl.BlockSpec((tm, two_d), lambda i: (i, 0)),
                  _resident((1, d_sgu)), _resident(ws.shape), _resident(b_exp.shape)],
        out_specs=pl.BlockSpec((tm, d_sgu), lambda i: (i, 0)),
        out_shape=jax.ShapeDtypeStruct((m, d_sgu), BF16),
        compiler_params=_cparams("parallel"),
        name="sgu",
    )(z, g.reshape(1, -1), ws, b_exp)


def _half_rms(x, gains):
    lane = lax.broadcasted_iota(jnp.int32, x.shape, 1)
    lo = lane < NSA_DH
    x2 = x * x
    s_lo = jnp.sum(jnp.where(lo, x2, 0.0), axis=-1, keepdims=True)
    s_hi = jnp.sum(jnp.where(lo, 0.0, x2), axis=-1, keepdims=True)
    inv = lax.rsqrt(jnp.where(lo, s_lo, s_hi) * (1.0 / NSA_DH) + EPS)
    return x * inv * gains


def _nsa_prep_kernel(q_ref, kv_ref, qg_ref, kg_ref, qo_ref, kt_ref, v_ref):
    i = pl.program_id(2)
    scale = 1.0 / math.sqrt(NSA_DH)
    q = q_ref[...]
    for hp in range(NSA_HPG // 2):
        sl = slice(hp * 2 * NSA_DH, (hp + 1) * 2 * NSA_DH)
        qo_ref[:, sl] = (_half_rms(q[:, sl], qg_ref[...]) * scale).astype(qo_ref.dtype)

    @pl.when(i == 0)
    def _():
        kt_ref[...] = jnp.zeros_like(kt_ref)
        v_ref[...] = jnp.zeros_like(v_ref)

    @pl.when(i > 0)
    def _():
        kv = kv_ref[...]
        kn = _half_rms(kv[:, :2 * NSA_DH], kg_ref[...])
        kt_ref[...] = kn.T.astype(kt_ref.dtype)
        v_ref[...] = kv[:, 2 * NSA_DH:].astype(v_ref.dtype)


def _nsa_prep(zq, zkvs, qg2, kg2, batch):
    m = zq.shape[0]
    t = m // batch
    tp = WINDOW
    per_b = t // tp
    gw = NSA_HPG * NSA_DH

    def in_map(bb, g, i):
        return (bb * per_b + jnp.maximum(i - 1, 0), g)

    return pl.pallas_call(
        _nsa_prep_kernel,
        grid=(batch, NSA_KV_GROUPS, per_b + 1),
        in_specs=[pl.BlockSpec((tp, gw), in_map), pl.BlockSpec((tp, gw), in_map),
                  _resident((1, 2 * NSA_DH)), _resident((1, 2 * NSA_DH))],
        out_specs=[pl.BlockSpec((tp, gw), in_map),
                   pl.BlockSpec((None, None, 2 * NSA_DH, tp), lambda bb, g, i: (bb, g, 0, i)),
                   pl.BlockSpec((None, None, tp, 2 * NSA_DH), lambda bb, g, i: (bb, g, i, 0))],
        out_shape=[jax.ShapeDtypeStruct((m, NSA_KV_GROUPS * gw), BF16),
                   jax.ShapeDtypeStruct((batch, NSA_KV_GROUPS, 2 * NSA_DH, WINDOW + t), BF16),
                   jax.ShapeDtypeStruct((batch, NSA_KV_GROUPS, WINDOW + t, 2 * NSA_DH), BF16)],
        compiler_params=_cparams("parallel", "parallel", "arbitrary"),
        name="nsa_prep",
    )(zq, zkvs, qg2, kg2)


def _nsa_cmp_kernel(xk_ref, xv_ref, pe_ref, w1_ref, w2_ref, kg_ref, kct_ref, vc_ref):
    half = CMP_STRIDE * NSA_DH
    rows = xk_ref.shape[0]
    rid = lax.broadcasted_iota(jnp.int32, (rows, 1), 0)

    def mlp(x_ref, kind):
        x = x_ref[...].astype(BF16)
        w1 = w1_ref[kind].astype(BF16)
        first = _dot(x, w1[:half])
        second = _dot(x, w1[half:])
        second = jnp.where(rid < rows - 1, pltpu.roll(second, rows - 1, 0), 0.0)
        pe = jnp.broadcast_to(pe_ref[kind:kind + 1, :], (8, 2 * half)).astype(BF16)
        hid = first + second + _dot(pe, w1)[0:1]
        hid = (hid * jax.nn.sigmoid(hid)).astype(BF16)
        return _dot(hid, w2_ref[kind].astype(BF16))

    kc = mlp(xk_ref, 0)
    kct_ref[...] = _rms(kc, kg_ref[...]).T.astype(kct_ref.dtype)
    vc_ref[...] = mlp(xv_ref, 1).astype(vc_ref.dtype)


def _nsa_cmp(x16, pe, w1, w2, kg0):
    batch, _, g, rows, width = x16.shape
    return pl.pallas_call(
        _nsa_cmp_kernel,
        grid=(batch, g),
        in_specs=[pl.BlockSpec((None, None, None, rows, width), lambda bb, gg: (bb, 0, gg, 0, 0)),
                  pl.BlockSpec((None, None, None, rows, width), lambda bb, gg: (bb, 1, gg, 0, 0)),
                  _resident(pe.shape), _resident(w1.shape), _resident(w2.shape), _resident((1, NSA_DH))],
        out_specs=[pl.BlockSpec((None, None, NSA_DH, rows), lambda bb, gg: (bb, gg, 0, 0)),
                   pl.BlockSpec((None, None, rows, NSA_DH), lambda bb, gg: (bb, gg, 0, 0))],
        out_shape=[jax.ShapeDtypeStruct((batch, g, NSA_DH, rows), BF16),
                   jax.ShapeDtypeStruct((batch, g, rows, NSA_DH), BF16)],
        compiler_params=_cparams("parallel", "parallel"),
        name="nsa_cmp",
    )(x16, x16, pe, w1, w2, kg0.reshape(1, NSA_DH))


def _split3_bf16(x):
    hi = x.astype(BF16)
    r1 = x - hi.astype(F32)
    mid = r1.astype(BF16)
    lo = (r1 - mid.astype(F32)).astype(BF16)
    return hi, mid, lo


def _sel_block_of(pos):
    return lax.shift_right_logical(pos, int(math.log2(SEL_BLOCK)))


def _nsa_attn_kernel(slopes_ref, q_ref, kt_ref, v_ref, kct_ref, vc_ref, gate_ref, o_ref):
    g = pl.program_id(1)
    c = pl.program_id(2)
    n_cmp_rows = kct_ref.shape[1]
    n_sel = (kt_ref.shape[1] - WINDOW) // SEL_BLOCK
    ck = SEL_KEY_CHUNK
    wk = WINDOW + Q_BLOCK

    q = q_ref[...]
    qs = jnp.concatenate([q[:, h * NSA_DH:(h + 1) * NSA_DH] for h in range(NSA_HPG)], axis=0)
    slopes = [slopes_ref[g * NSA_HPG + h] for h in range(NSA_HPG)]
    t_col = c * Q_BLOCK + lax.broadcasted_iota(jnp.int32, (Q_BLOCK, 1), 0)

    def head_rows(x, h):
        return x[h * Q_BLOCK:(h + 1) * Q_BLOCK]

    n_idx = lax.broadcasted_iota(jnp.int32, (1, n_cmp_rows), 1)
    d_c = t_col - (CMP_STRIDE * n_idx + CMP_LEN - 1)
    m_c = (d_c >= 0) & (n_idx < n_cmp_rows - 1)
    d_cf = d_c.astype(F32)
    s_c = _dot(qs, kct_ref[...])
    vc = vc_ref[...]
    o_cmp = []
    p_sum = jnp.zeros((Q_BLOCK, n_cmp_rows), F32)
    for h in range(NSA_HPG):
        s = jnp.where(m_c, head_rows(s_c, h) - slopes[h] * d_cf, NEG_INF)
        e = jnp.exp(s - jnp.max(s, axis=-1, keepdims=True))
        p = jnp.where(m_c, e / jnp.sum(e, axis=-1, keepdims=True), 0.0)
        o_cmp.append(_dot(p.astype(BF16), vc))
        p_sum = p_sum + p

    j_row = lax.broadcasted_iota(jnp.int32, (n_sel, 1), 0)
    n_lane = lax.broadcasted_iota(jnp.int32, (1, n_cmp_rows), 1)
    ratio = SEL_BLOCK // CMP_STRIDE
    first_cmp = ratio * j_row - (CMP_LEN - 1) // CMP_STRIDE
    overlap_t = ((n_lane >= first_cmp) & (n_lane < ratio * (j_row + 1)) & (n_lane < n_cmp_rows - 1))
    overlap_t = overlap_t.astype(F32).astype(BF16)
    imp_t = sum(_dot_nt(overlap_t, part) for part in _split3_bf16(p_sum))
    t_lane = c * Q_BLOCK + lax.broadcasted_iota(jnp.int32, (1, Q_BLOCK), 1)
    cur = _sel_block_of(t_lane)
    forced = (j_row == 0) | (j_row == cur) | (j_row == cur - 1)
    valid = SEL_BLOCK * j_row <= t_lane
    val = jnp.where(valid, jnp.where(forced, FORCE_SCORE, imp_t), NEG_INF)
    rank = jnp.zeros((n_sel, Q_BLOCK), F32)
    for i in range(n_sel):
        vi = val[i:i + 1, :]
        ahead = (vi > val) | ((vi == val) & (j_row > i))
        rank = rank + ahead.astype(F32)
    sel_t = ((rank < TOP_N) & valid).astype(F32)
    sel = sel_t.T.astype(BF16)

    r_minus_col = (lax.broadcasted_iota(jnp.int32, (Q_BLOCK, ck), 0)
                   - lax.broadcasted_iota(jnp.int32, (Q_BLOCK, ck), 1))
    blk_row = lax.broadcasted_iota(jnp.int32, (n_sel, ck), 0)
    blk_col = lax.broadcasted_iota(jnp.int32, (n_sel, ck), 1)

    def sel_step(i, carry):
        m_run, l_run, acc = carry
        k0 = pl.multiple_of(i * ck, ck)
        kblk = kt_ref[0:NSA_DH, pl.ds(WINDOW + k0, ck)]
        vblk = v_ref[pl.ds(WINDOW + k0, ck), :]
        s = _dot(qs, kblk)
        expand = (_sel_block_of(blk_col + k0) == blk_row).astype(F32).astype(BF16)
        chosen = _dot(sel, expand)
        d = r_minus_col + (c * Q_BLOCK - k0)
        ok = (chosen > 0.5) & (d >= 0)
        d_f = d.astype(F32)
        s = jnp.concatenate([jnp.where(ok, head_rows(s, h) - slopes[h] * d_f, NEG_INF)
                             for h in range(NSA_HPG)], axis=0)
        m_new = jnp.maximum(m_run, jnp.max(s, axis=-1, keepdims=True))
        alpha = jnp.exp(m_run - m_new)
        p = jnp.exp(s - m_new)
        l_new = alpha * l_run + jnp.sum(p, axis=-1, keepdims=True)
        acc_new = alpha * acc + _dot(p.astype(BF16), vblk)
        return m_new, l_new, acc_new

    n_chunks = (c * Q_BLOCK + Q_BLOCK + ck - 1) // ck
    rows = NSA_HPG * Q_BLOCK
    _, l_fin, acc = lax.fori_loop(
        0, n_chunks, sel_step,
        (jnp.full((rows, 1), NEG_INF, F32), jnp.zeros((rows, 1), F32), jnp.zeros((rows, 2 * NSA_DH), F32)))
    o_sel = acc[:, :NSA_DH] / l_fin

    w0 = pl.multiple_of(c * Q_BLOCK, Q_BLOCK)
    kw = kt_ref[NSA_DH:2 * NSA_DH, pl.ds(w0, wk)]
    vw = v_ref[pl.ds(w0, wk), :]
    s_w = _dot(qs, kw)
    pos_w = c * Q_BLOCK - WINDOW + lax.broadcasted_iota(jnp.int32, (1, wk), 1)
    d_w = t_col - pos_w
    ok_w = (d_w >= 0) & (d_w < WINDOW) & (pos_w >= 0)
    d_wf = d_w.astype(F32)
    s_w = jnp.concatenate([jnp.where(ok_w, head_rows(s_w, h) - slopes[h] * d_wf, NEG_INF)
                           for h in range(NSA_HPG)], axis=0)
    e_w = jnp.exp(s_w - jnp.max(s_w, axis=-1, keepdims=True))
    o_win = _dot(e_w.astype(BF16), vw)[:, NSA_DH:] / jnp.sum(e_w, axis=-1, keepdims=True)

    gates = jax.nn.sigmoid(gate_ref[...])
    for h in range(NSA_HPG):
        out = (gates[:, 3 * h:3 * h + 1] * o_cmp[h]
               + gates[:, 3 * h + 1:3 * h + 2] * head_rows(o_sel, h)
               + gates[:, 3 * h + 2:3 * h + 3] * head_rows(o_win, h))
        o_ref[:, h * NSA_DH:(h + 1) * NSA_DH] = out.astype(o_ref.dtype)


def _nsa_attn(slopes, qn, kt, v, kct, vc, zgate, batch):
    m = qn.shape[0]
    t = m // batch
    n_qb = t // Q_BLOCK
    gw = NSA_HPG * NSA_DH
    n_cmp_rows = kct.shape[-1]

    def row_map(bb, g, c):
        return (bb * n_qb + c, g)

    def kv_map(bb, g, c):
        return (bb, g, 0, 0)

    return pl.pallas_call(
        _nsa_attn_kernel,
        grid=(batch, NSA_KV_GROUPS, n_qb),
        in_specs=[pl.BlockSpec(memory_space=pltpu.SMEM),
                  pl.BlockSpec((Q_BLOCK, gw), row_map),
                  pl.BlockSpec((None, None, 2 * NSA_DH, WINDOW + t), kv_map),
                  pl.BlockSpec((None, None, WINDOW + t, 2 * NSA_DH), kv_map),
                  pl.BlockSpec((None, None, NSA_DH, n_cmp_rows), kv_map),
                  pl.BlockSpec((None, None, n_cmp_rows, NSA_DH), kv_map),
                  pl.BlockSpec((Q_BLOCK, 128), row_map)],
        out_specs=pl.BlockSpec((Q_BLOCK, gw), row_map),
        out_shape=jax.ShapeDtypeStruct((m, NSA_KV_GROUPS * gw), BF16),
        compiler_params=_cparams("parallel", "parallel", "arbitrary"),
        name="nsa_attn",
    )(slopes, qn, kt, v, kct, vc, zgate)


def _merge_kernel(h_ref, a_ref, b_ref, c_ref, zm_ref, wa_ref, wb_ref, wc_ref, wo_ref, o_ref):
    d = h_ref.shape[1]
    gates = jax.nn.sigmoid(zm_ref[...])
    y = (gates[:, :d] * _dot(a_ref[...], wa_ref[...])
         + gates[:, d:2 * d] * _dot(b_ref[...], wb_ref[...])
         + gates[:, 2 * d:] * _dot(c_ref[...], wc_ref[...]))
    o_ref[...] = h_ref[...] + _dot(y.astype(BF16), wo_ref[...])


def _merge(h, a, b, c, zm, wa, wb, wc, wo):
    m, d = h.shape
    tm = ROW_TILE

    def rows(n):
        return pl.BlockSpec((tm, n), lambda i: (i, 0))

    return pl.pallas_call(
        _merge_kernel,
        grid=(m // tm,),
        in_specs=[rows(d), rows(a.shape[1]), rows(b.shape[1]), rows(c.shape[1]), rows(zm.shape[1]),
                  _resident(wa.shape), _resident(wb.shape), _resident(wc.shape), _resident(wo.shape)],
        out_specs=rows(d),
        out_shape=jax.ShapeDtypeStruct((m, d), F32),
        compiler_params=_cparams("parallel"),
        name="merge",
    )(h, a, b, c, zm, wa, wb, wc, wo)


def _xkv_kernel(mem_ref, g_ref, w_ref, kg_ref, k_ref, v_ref):
    mn = _rms(mem_ref[...], g_ref[...]).astype(BF16)
    kv = _dot(mn, w_ref[...])
    dk = k_ref.shape[1]
    for h in range(X_HEADS):
        sl = slice(h * X_DH, (h + 1) * X_DH)
        k_ref[:, sl] = _rms(kv[:, sl], kg_ref[...]).astype(k_ref.dtype)
    v_ref[...] = kv[:, dk:].astype(v_ref.dtype)


def _xkv(mem2, g, w, kg, batch):
    rows, d = mem2.shape
    mlen = rows // batch
    dk = X_HEADS * X_DH
    return pl.pallas_call(
        _xkv_kernel,
        grid=(batch,),
        in_specs=[pl.BlockSpec((mlen, d), lambda bb: (bb, 0)),
                  _resident((1, d)), _resident(w.shape), _resident((1, X_DH))],
        out_specs=[pl.BlockSpec((mlen, dk), lambda bb: (bb, 0)), pl.BlockSpec((mlen, dk), lambda bb: (bb, 0))],
        out_shape=[jax.ShapeDtypeStruct((rows, dk), BF16), jax.ShapeDtypeStruct((rows, dk), BF16)],
        compiler_params=_cparams("parallel"),
        name="xattn_kv",
    )(mem2, g.reshape(1, d), w, kg.reshape(1, X_DH))


def _xattn_kernel(h_ref, g_ref, wq_ref, qg_ref, k_ref, v_ref, wo_ref, o_ref):
    h = h_ref[...]
    q = _dot(_rms(h, g_ref[...]).astype(BF16), wq_ref[...])
    scale = 1.0 / math.sqrt(X_DH)
    outs = []
    for hd in range(X_HEADS):
        sl = slice(hd * X_DH, (hd + 1) * X_DH)
        qh = _rms(q[:, sl], qg_ref[...]).astype(BF16)
        s = _dot_nt(qh, k_ref[:, sl]) * scale
        e = jnp.exp(s - jnp.max(s, axis=-1, keepdims=True))
        p = e / jnp.sum(e, axis=-1, keepdims=True)
        outs.append(_dot(p.astype(BF16), v_ref[:, sl]))
    o = jnp.concatenate(outs, axis=-1).astype(BF16)
    o_ref[...] = h + _dot(o, wo_ref[...])


def _xattn(h, g, wq, qg, k, v, wo, batch):
    m, d = h.shape
    tm = ROW_TILE
    per_b = m // batch // tm
    mlen = k.shape[0] // batch
    dk = k.shape[1]
    return pl.pallas_call(
        _xattn_kernel,
        grid=(batch, per_b),
        in_specs=[pl.BlockSpec((tm, d), lambda bb, i: (bb * per_b + i, 0)),
                  _resident((1, d)), _resident(wq.shape), _resident((1, X_DH)),
                  pl.BlockSpec((mlen, dk), lambda bb, i: (bb, 0)),
                  pl.BlockSpec((mlen, dk), lambda bb, i: (bb, 0)),
                  _resident(wo.shape)],
        out_specs=pl.BlockSpec((tm, d), lambda bb, i: (bb * per_b + i, 0)),
        out_shape=jax.ShapeDtypeStruct((m, d), F32),
        compiler_params=_cparams("parallel", "parallel"),
        name="xattn",
    )(h, g.reshape(1, d), wq, qg.reshape(1, X_DH), k, v, wo)


def _inproj_layout(d_conv, d_sgu):
    cols_q = NSA_HEADS * NSA_DH
    kv0 = 2 * d_conv + 2 * d_sgu + cols_q
    gdh = NSA_KV_GROUPS * NSA_DH
    gate0 = kv0 + 6 * gdh
    merge0 = gate0 + NSA_HEADS * 3
    d = np.arange(NSA_DH)
    kvs = np.concatenate([kv0 + i * gdh + g * NSA_DH + d
                          for g in range(NSA_KV_GROUPS) for i in (2, 4, 3, 5)])
    cmp = kv0 + np.arange(2 * gdh)
    return kv0, kvs, cmp, gate0, merge0


def _prep_w_in(w_in, d_conv, d_sgu):
    kv0, kvs, cmp, gate0, merge0 = _inproj_layout(d_conv, d_sgu)
    per_g = NSA_HPG * 3
    gate_blocks = []
    for g in range(NSA_KV_GROUPS):
        blk = w_in[:, :, gate0 + g * per_g:gate0 + (g + 1) * per_g]
        gate_blocks.append(jnp.pad(blk, ((0, 0), (0, 0), (0, 128 - per_g))))
    w_all = jnp.concatenate([w_in[:, :, :kv0], w_in[:, :, kvs], w_in[:, :, cmp]] + gate_blocks
                            + [w_in[:, :, merge0:]], axis=-1)
    d_model = w_in.shape[1]
    widths = (2 * d_conv, 2 * d_sgu, NSA_HEADS * NSA_DH, len(kvs), len(cmp), 128 * NSA_KV_GROUPS, 3 * d_model)
    return w_all.astype(BF16), widths


def kernel(x, mem, norm_ffn1, ffn1_wi, ffn1_wo, norm_mix, w_in, conv_w, conv_b, conv_ln_g, conv_ln_b, conv_out,
           sgu_norm, sgu_ws, sgu_b, sgu_out, nsa_q_norm, nsa_k_norm, cmp_pe, cmp_w1, cmp_w2, nsa_out, w_out,
           norm_xattn, mem_norm, xq, xkv, xq_norm, xk_norm, xo, norm_ffn2, ffn2_wi, ffn2_wo):
    batch, t, d_model = x.shape
    depth = w_in.shape[0]
    d_conv = conv_w.shape[-1]
    d_sgu = sgu_norm.shape[-1]
    m = batch * t

    w_all, widths = _prep_w_in(w_in, d_conv, d_sgu)
    bf = lambda w: w.astype(BF16)
    ffn1_wi, ffn1_wo, ffn2_wi, ffn2_wo = bf(ffn1_wi), bf(ffn1_wo), bf(ffn2_wi), bf(ffn2_wo)
    conv_out, sgu_out, nsa_out, w_out = bf(conv_out), bf(sgu_out), bf(nsa_out), bf(w_out)
    xq, xkv, xo = bf(xq), bf(xkv), bf(xo)
    sgu_b_exp = jnp.repeat(jnp.swapaxes(sgu_b, 1, 2), d_sgu // SGU_GROUPS, axis=2)
    slopes = jnp.asarray(2.0 ** (-8.0 * np.arange(1, NSA_HEADS + 1) / NSA_HEADS), dtype=F32)
    pe_flat = cmp_pe.reshape(depth, 2, CMP_LEN * NSA_DH)

    h = x.reshape(m, d_model)
    mem2 = mem.reshape(-1, d_model)
    for l in range(depth):
        h = _ffn(h, norm_ffn1[l], ffn1_wi[l], ffn1_wo[l])

        z_conv, z_sgu, z_q, z_kvs, z_cmp, z_gate, z_merge = _inproj(h, norm_mix[l], w_all[l], widths)
        act_a = _conv(z_conv, conv_w[l], conv_b[l], conv_ln_g[l], conv_ln_b[l], batch)
        act_b = _sgu(z_sgu, sgu_norm[l], sgu_ws[l], sgu_b_exp[l])

        qg2 = jnp.tile(nsa_q_norm[l], 2).reshape(1, 2 * NSA_DH)
        kg2 = jnp.concatenate([nsa_k_norm[l, 1], nsa_k_norm[l, 2]]).reshape(1, 2 * NSA_DH)
        qn, kt, v = _nsa_prep(z_q, z_kvs, qg2, kg2, batch)
        x16 = (z_cmp.reshape(batch, t // CMP_STRIDE, CMP_STRIDE, 2, NSA_KV_GROUPS, NSA_DH)
               .transpose(0, 3, 4, 1, 2, 5)
               .reshape(batch, 2, NSA_KV_GROUPS, t // CMP_STRIDE, CMP_STRIDE * NSA_DH))
        kct, vc = _nsa_cmp(x16, pe_flat[l], cmp_w1[l], cmp_w2[l], nsa_k_norm[l, 0])
        act_c = _nsa_attn(slopes, qn, kt, v, kct, vc, z_gate, batch)

        h = _merge(h, act_a, act_b, act_c, z_merge, conv_out[l], sgu_out[l], nsa_out[l], w_out[l])

        xk, xv = _xkv(mem2, mem_norm[l], xkv[l], xk_norm[l], batch)
        h = _xattn(h, norm_xattn[l], xq[l], xq_norm[l], xk, xv, xo[l], batch)

        h = _ffn(h, norm_ffn2[l], ffn2_wi[l], ffn2_wo[l])
    return h.reshape(batch, t, d_model)
```

```python
import functools
import math

import numpy as np
import jax
import jax.numpy as jnp
from jax import lax
from jax.experimental import pallas as pl
from jax.experimental.pallas import tpu as pltpu

F32 = jnp.float32
BF16 = jnp.bfloat16

EPS = 1e-6
NEG_INF = -1e30
FORCE_SCORE = 1e4
LOG2E = math.log2(math.e)

CONV_WIDTH = 31
CONV_HALO = 32
SGU_CHUNK = 128
SGU_GROUPS = 4
NSA_HEADS = 8
NSA_KV_GROUPS = 2
NSA_HPG = NSA_HEADS // NSA_KV_GROUPS
NSA_DH = 64
CMP_LEN = 32
CMP_STRIDE = 16
SEL_BLOCK = 64
TOP_N = 16
WINDOW = 512
Q_BLOCK = 128
X_HEADS = 4
X_DH = 128

AUG_W = 128
AUG_A = NSA_DH
AUG_B = NSA_DH + 3
AUG_N = NSA_DH + 6
AUG_FLAG = NSA_DH + 9
MASK_OFF = -(2.0 ** 100)
V_ROWS = NSA_DH + 16

ROW_TILE = 512
INPROJ_ROW_TILE = 256
SEL_KEY_CHUNK = 256
VMEM_LIMIT = 56 * 1024 * 1024


def _cparams(*sem):
    return pltpu.CompilerParams(dimension_semantics=sem, vmem_limit_bytes=VMEM_LIMIT)


def _resident(shape):
    nd = len(shape)
    return pl.BlockSpec(shape, lambda *_: (0,) * nd, pipeline_mode=pl.Buffered(1))


def _rms(x, g):
    return x * lax.rsqrt(jnp.mean(x * x, axis=-1, keepdims=True) + EPS) * g


def _dot(a, b):
    return jnp.dot(a, b, preferred_element_type=F32)


def _dot_nt(a, b):
    return lax.dot_general(a, b, (((1,), (1,)), ((), ())), preferred_element_type=F32)


def _ffn_kernel(x_ref, g_ref, wi_ref, wo_ref, o_ref, *, d_ff, n_chunks):
    x = x_ref[...]
    xn = _rms(x, g_ref[...]).astype(BF16)
    ck = d_ff // n_chunks
    acc = x
    for c in range(n_chunks):
        a = _dot(xn, wi_ref[:, c * ck:(c + 1) * ck])
        b = _dot(xn, wi_ref[:, d_ff + c * ck:d_ff + (c + 1) * ck])
        mid = (a * jax.nn.sigmoid(a) * b).astype(BF16)
        acc = acc + 0.5 * _dot(mid, wo_ref[c * ck:(c + 1) * ck, :])
    o_ref[...] = acc


def _ffn(h, g, wi, wo):
    m, d = h.shape
    d_ff = wo.shape[0]
    tm = ROW_TILE
    return pl.pallas_call(
        functools.partial(_ffn_kernel, d_ff=d_ff, n_chunks=2),
        grid=(m // tm,),
        in_specs=[pl.BlockSpec((tm, d), lambda i: (i, 0)),
                  _resident((1, d)), _resident(wi.shape), _resident(wo.shape)],
        out_specs=pl.BlockSpec((tm, d), lambda i: (i, 0)),
        out_shape=jax.ShapeDtypeStruct((m, d), F32),
        compiler_params=_cparams("parallel"),
        name="ffn",
    )(h, g.reshape(1, d), wi, wo)


def _inproj_kernel(x_ref, g_ref, w_ref, *o_refs):
    xn = _rms(x_ref[...], g_ref[...]).astype(BF16)
    off = 0
    for o_ref in o_refs:
        n = o_ref.shape[1]
        o_ref[...] = _dot(xn, w_ref[:, off:off + n]).astype(o_ref.dtype)
        off += n


def _inproj(h, g, w, widths):
    m, d = h.shape
    tm = INPROJ_ROW_TILE
    return pl.pallas_call(
        _inproj_kernel,
        grid=(m // tm,),
        in_specs=[pl.BlockSpec((tm, d), lambda i: (i, 0)), _resident((1, d)), _resident(w.shape)],
        out_specs=[pl.BlockSpec((tm, n), lambda i: (i, 0)) for n in widths],
        out_shape=[jax.ShapeDtypeStruct((m, n), F32) for n in widths],
        compiler_params=_cparams("parallel"),
        name="inproj",
    )(h, g.reshape(1, d), w)


def _conv_kernel(cur_ref, prev_ref, w_ref, b_ref, lg_ref, lb_ref, o_ref, buf_ref, *, d_conv):
    i = pl.program_id(1)
    tt = cur_ref.shape[0]
    cur = cur_ref[...]
    prev = prev_ref[...]
    keep = (i > 0).astype(F32)
    buf_ref[0:CONV_HALO, :] = prev[:, :d_conv] * jax.nn.sigmoid(prev[:, d_conv:]) * keep
    buf_ref[CONV_HALO:, :] = cur[:, :d_conv] * jax.nn.sigmoid(cur[:, d_conv:])
    base = CONV_HALO - (CONV_WIDTH - 1)
    acc = jnp.zeros((tt, d_conv), F32)
    for k in range(CONV_WIDTH):
        acc = acc + w_ref[k:k + 1, :] * buf_ref[base + k:base + k + tt, :]
    y = acc + b_ref[...]
    mu = jnp.mean(y, axis=-1, keepdims=True)
    yc = y - mu
    var = jnp.mean(yc * yc, axis=-1, keepdims=True)
    yn = yc * lax.rsqrt(var + EPS) * lg_ref[...] + lb_ref[...]
    o_ref[...] = (yn * jax.nn.sigmoid(yn)).astype(o_ref.dtype)


def _conv(z, w, b, lg, lb, batch):
    m, two_d = z.shape
    d_conv = two_d // 2
    t = m // batch
    tt = ROW_TILE
    per_b = t // tt
    halo_per_tile = tt // CONV_HALO
    return pl.pallas_call(
        functools.partial(_conv_kernel, d_conv=d_conv),
        grid=(batch, per_b),
        in_specs=[pl.BlockSpec((tt, two_d), lambda bb, i: (bb * per_b + i, 0)),
                  pl.BlockSpec((CONV_HALO, two_d),
                               lambda bb, i: (jnp.maximum((bb * per_b + i) * halo_per_tile - 1, 0), 0)),
                  _resident(w.shape), _resident((1, d_conv)), _resident((1, d_conv)), _resident((1, d_conv))],
        out_specs=pl.BlockSpec((tt, d_conv), lambda bb, i: (bb * per_b + i, 0)),
        out_shape=jax.ShapeDtypeStruct((m, d_conv), BF16),
        scratch_shapes=[pltpu.VMEM((CONV_HALO + tt, d_conv), F32)],
        compiler_params=_cparams("parallel", "parallel"),
        name="conv",
    )(z, z, w, b.reshape(1, -1), lg.reshape(1, -1), lb.reshape(1, -1))


def _sgu_kernel(z_ref, g_ref, ws_ref, b_ref, o_ref, *, d_sgu):
    tm = z_ref.shape[0]
    gd = d_sgu // SGU_GROUPS
    ge = jax.nn.gelu(z_ref[...])
    u = ge[:, :d_sgu]
    vn = _rms(ge[:, d_sgu:], g_ref[...]).astype(BF16)
    row = lax.broadcasted_iota(jnp.int32, (SGU_CHUNK, SGU_CHUNK), 0)
    col = lax.broadcasted_iota(jnp.int32, (SGU_CHUNK, SGU_CHUNK), 1)
    for gr in range(SGU_GROUPS):
        wt = jnp.where(row >= col, ws_ref[gr], 0.0).astype(BF16)
        for ch in range(tm // SGU_CHUNK):
            rows = slice(ch * SGU_CHUNK, (ch + 1) * SGU_CHUNK)
            cols = slice(gr * gd, (gr + 1) * gd)
            mixed = _dot(wt, vn[rows, cols]) + b_ref[:, cols]
            o_ref[rows, cols] = (u[rows, cols] * mixed).astype(o_ref.dtype)


def _sgu(z, g, ws, b_exp):
    m, two_d = z.shape
    d_sgu = two_d // 2
    tm = ROW_TILE
    return pl.pallas_call(
        functools.partial(_sgu_kernel, d_sgu=d_sgu),
        grid=(m // tm,),
        in_specs=[pl.BlockSpec((tm, two_d), lambda i: (i, 0)),
                  _resident((1, d_sgu)), _resident(ws.shape), _resident(b_exp.shape)],
        out_specs=pl.BlockSpec((tm, d_sgu), lambda i: (i, 0)),
        out_shape=jax.ShapeDtypeStruct((m, d_sgu), BF16),
        compiler_params=_cparams("parallel"),
        name="sgu",
    )(z, g.reshape(1, -1), ws, b_exp)


def _half_rms(x, gains):
    lane = lax.broadcasted_iota(jnp.int32, x.shape, 1)
    lo = lane < NSA_DH
    x2 = x * x
    s_lo = jnp.sum(jnp.where(lo, x2, 0.0), axis=-1, keepdims=True)
    s_hi = jnp.sum(jnp.where(lo, 0.0, x2), axis=-1, keepdims=True)
    inv = lax.rsqrt(jnp.where(lo, s_lo, s_hi) * (1.0 / NSA_DH) + EPS)
    return x * inv * gains


def _nsa_prep_kernel(q_ref, kv_ref, qg_ref, kg_ref, ex_ref, qt_ref, ks_ref, kw_ref, vst_ref, vwt_ref):
    g = pl.program_id(1)
    i = pl.program_id(2)
    tp = q_ref.shape[0]
    pair_w = 2 * NSA_DH
    lane = lax.broadcasted_iota(jnp.int32, (tp, AUG_W), 1)
    head_lanes = lane < NSA_DH
    is_pad = i == 0

    q = q_ref[...]
    q_scale = LOG2E / math.sqrt(NSA_DH)
    for hp in range(NSA_HPG // 2):
        pair = _half_rms(q[:, hp * pair_w:(hp + 1) * pair_w], qg_ref[...]) * q_scale
        for sub in range(2):
            h = 2 * hp + sub
            x = pair if sub == 0 else pltpu.roll(pair, NSA_DH, 1)
            qa = jnp.where(head_lanes, x, ex_ref[pl.ds(g * NSA_HPG + h, 1), :])
            qat = qa.T.astype(qt_ref.dtype)
            for cb in range(tp // Q_BLOCK):
                col = (cb * NSA_HPG + h) * Q_BLOCK
                qt_ref[:, col:col + Q_BLOCK] = qat[:, cb * Q_BLOCK:(cb + 1) * Q_BLOCK]

    kv = kv_ref[...]
    kpair = _half_rms(kv[:, :pair_w], kg_ref[...])
    pos = (i - 1) * tp + lax.broadcasted_iota(jnp.int32, (tp, 1), 0)
    blk = lax.shift_right_arithmetic(pos, int(math.log2(SEL_BLOCK))).astype(F32)
    off = (pos & (SEL_BLOCK - 1)).astype(F32)
    aug = jnp.where((lane >= AUG_A) & (lane < AUG_A + 3), blk,
                    jnp.where((lane >= AUG_B) & (lane < AUG_B + 3), off, 0.0))
    pad_key = jnp.where(lane == AUG_FLAG, MASK_OFF, 0.0)
    ks_ref[...] = jnp.where(head_lanes, kpair, aug).astype(ks_ref.dtype)
    kw = jnp.where(head_lanes, pltpu.roll(kpair, NSA_DH, 1), aug)
    kw_ref[...] = jnp.where(is_pad, pad_key, kw).astype(kw_ref.dtype)

    vpair = kv[:, pair_w:]
    ones_lane = jnp.where(lane == NSA_DH, 1.0, 0.0)
    vst_ref[...] = jnp.where(head_lanes, vpair, ones_lane).T[:V_ROWS].astype(vst_ref.dtype)
    vw = jnp.where(head_lanes, pltpu.roll(vpair, NSA_DH, 1), ones_lane)
    vwt_ref[...] = jnp.where(is_pad, 0.0, vw).T[:V_ROWS].astype(vwt_ref.dtype)


def _nsa_prep(zq, zkvs, qg2, kg2, q_extras, batch):
    m = zq.shape[0]
    t = m // batch
    tp = WINDOW
    per_b = t // tp
    gw = NSA_HPG * NSA_DH
    g_n = NSA_KV_GROUPS

    def in_map(bb, g, i):
        return (bb * per_b + jnp.maximum(i - 1, 0), g)

    def rows_now(bb, g, i):
        return (bb, g, jnp.maximum(i - 1, 0), 0)

    def cols_now(bb, g, i):
        return (bb, g, 0, jnp.maximum(i - 1, 0))

    return pl.pallas_call(
        _nsa_prep_kernel,
        grid=(batch, g_n, per_b + 1),
        in_specs=[pl.BlockSpec((tp, gw), in_map), pl.BlockSpec((tp, gw), in_map),
                  _resident((1, 2 * NSA_DH)), _resident((1, 2 * NSA_DH)), _resident(q_extras.shape)],
        out_specs=[pl.BlockSpec((None, None, AUG_W, tp * NSA_HPG), cols_now),
                   pl.BlockSpec((None, None, tp, AUG_W), rows_now),
                   pl.BlockSpec((None, None, tp, AUG_W), lambda bb, g, i: (bb, g, i, 0)),
                   pl.BlockSpec((None, None, V_ROWS, tp), cols_now),
                   pl.BlockSpec((None, None, V_ROWS, tp), lambda bb, g, i: (bb, g, 0, i))],
        out_shape=[jax.ShapeDtypeStruct((batch, g_n, AUG_W, t * NSA_HPG), BF16),
                   jax.ShapeDtypeStruct((batch, g_n, t, AUG_W), BF16),
                   jax.ShapeDtypeStruct((batch, g_n, WINDOW + t, AUG_W), BF16),
                   jax.ShapeDtypeStruct((batch, g_n, V_ROWS, t), BF16),
                   jax.ShapeDtypeStruct((batch, g_n, V_ROWS, WINDOW + t), BF16)],
        compiler_params=_cparams("parallel", "parallel", "arbitrary"),
        name="nsa_prep",
    )(zq, zkvs, qg2, kg2, q_extras)


def _nsa_cmp_kernel(xk_ref, xv_ref, pe_ref, w1_ref, w2_ref, kg_ref, kc_ref, vct_ref):
    half = CMP_STRIDE * NSA_DH
    rows = xk_ref.shape[0]
    rid = lax.broadcasted_iota(jnp.int32, (rows, 1), 0)

    def mlp(x_ref, kind):
        x = x_ref[...].astype(BF16)
        w1 = w1_ref[kind].astype(BF16)
        first = _dot(x, w1[:half])
        second = _dot(x, w1[half:])
        second = jnp.where(rid < rows - 1, pltpu.roll(second, rows - 1, 0), 0.0)
        pe = jnp.broadcast_to(pe_ref[kind:kind + 1, :], (8, 2 * half)).astype(BF16)
        hid = first + second + _dot(pe, w1)[0:1]
        hid = (hid * jax.nn.sigmoid(hid)).astype(BF16)
        return _dot(hid, w2_ref[kind].astype(BF16))

    lane = lax.broadcasted_iota(jnp.int32, (rows, AUG_W), 1)
    head_lanes = lane < NSA_DH
    zeros = jnp.zeros((rows, AUG_W - NSA_DH), F32)
    kc = jnp.concatenate([_rms(mlp(xk_ref, 0), kg_ref[...]), zeros], axis=1)
    aug = jnp.where((lane >= AUG_N) & (lane < AUG_N + 3), rid.astype(F32), 0.0)
    kc_ref[...] = jnp.where(head_lanes, kc, aug).astype(kc_ref.dtype)
    vc = jnp.concatenate([mlp(xv_ref, 1), zeros], axis=1)
    vct_ref[...] = vc.T[:NSA_DH].astype(vct_ref.dtype)


def _nsa_cmp(x16, pe, w1, w2, kg0):
    batch, _, g, rows, width = x16.shape
    return pl.pallas_call(
        _nsa_cmp_kernel,
        grid=(batch, g),
        in_specs=[pl.BlockSpec((None, None, None, rows, width), lambda bb, gg: (bb, 0, gg, 0, 0)),
                  pl.BlockSpec((None, None, None, rows, width), lambda bb, gg: (bb, 1, gg, 0, 0)),
                  _resident(pe.shape), _resident(w1.shape), _resident(w2.shape), _resident((1, NSA_DH))],
        out_specs=[pl.BlockSpec((None, None, rows, AUG_W), lambda bb, gg: (bb, gg, 0, 0)),
                   pl.BlockSpec((None, None, NSA_DH, rows), lambda bb, gg: (bb, gg, 0, 0))],
        out_shape=[jax.ShapeDtypeStruct((batch, g, rows, AUG_W), BF16),
                   jax.ShapeDtypeStruct((batch, g, NSA_DH, rows), BF16)],
        compiler_params=_cparams("parallel", "parallel"),
        name="nsa_cmp",
    )(x16, x16, pe, w1, w2, kg0.reshape(1, NSA_DH))


def _split3_bf16(x):
    hi = x.astype(BF16)
    r1 = x - hi.astype(F32)
    mid = r1.astype(BF16)
    lo = (r1 - mid.astype(F32)).astype(BF16)
    return hi, mid, lo


def _top_n_mask(val, valid):
    n_sel = val.shape[0]
    groups = [val[8 * k:8 * k + 8] for k in range(n_sel // 8)]
    ranks = [jnp.zeros(g.shape, F32) for g in groups]
    sub = lax.broadcasted_iota(jnp.int32, groups[0].shape, 0)
    for i in range(n_sel):
        vi = jnp.broadcast_to(val[i:i + 1, :], groups[0].shape)
        for k, vk in enumerate(groups):
            if 8 * k > i:
                ahead = vi >= vk
            elif 8 * k + 7 < i:
                ahead = vi > vk
            else:
                ahead = (vi > vk) | ((vi == vk) & (sub + 8 * k > i))
            ranks[k] = ranks[k] + jnp.where(ahead, 1.0, 0.0)
    rank = jnp.concatenate(ranks, axis=0)
    return (rank < TOP_N) & valid


def _nsa_attn_kernel(qt_ref, ks_ref, kw_ref, vst_ref, vwt_ref, kc_ref, vct_ref, gate_ref, o_ref,
                     selneg_ref, sa_ref, sb_ref, m_ref, acc_ref):
    c = pl.program_id(2)
    nq = Q_BLOCK
    n_sel = ks_ref.shape[0] // SEL_BLOCK
    n_cmp_rows = kc_ref.shape[0]
    ck = SEL_KEY_CHUNK
    wk = WINDOW + nq
    qt = qt_ref[...]
    q_lane = lax.broadcasted_iota(jnp.int32, (1, nq), 1)
    t_lane = c * nq + q_lane

    def heads(x):
        return [x[:, h * nq:(h + 1) * nq] for h in range(NSA_HPG)]

    def add_per_head(x, bias):
        return jnp.concatenate([xh + bias for xh in heads(x)], axis=1)

    def normalised(acc):
        return acc[:NSA_DH] * (1.0 / acc[NSA_DH:NSA_DH + 1])

    w0 = pl.multiple_of(c * nq, nq)
    e_row = lax.broadcasted_iota(jnp.int32, (nq, 1), 0)
    s_w = _dot(kw_ref[pl.ds(w0, wk), :], qt)
    s_w = jnp.concatenate([add_per_head(s_w[:nq], jnp.where(e_row > q_lane, 0.0, NEG_INF)),
                           s_w[nq:WINDOW],
                           add_per_head(s_w[WINDOW:], jnp.where(e_row <= q_lane, 0.0, NEG_INF))], axis=0)
    p_w = jnp.exp2(s_w - jnp.max(s_w, axis=0, keepdims=True)).astype(BF16)
    o_win = normalised(_dot(vwt_ref[:, pl.ds(w0, wk)], p_w))

    n_row = lax.broadcasted_iota(jnp.int32, (n_cmp_rows, 1), 0)
    ok_c = CMP_STRIDE * n_row + (CMP_LEN - 1) <= t_lane
    any_c = jnp.where(t_lane >= CMP_LEN - 1, 1.0, 0.0)
    s_c = add_per_head(_dot(kc_ref[...], qt), jnp.where(ok_c, 0.0, NEG_INF))
    e_c = jnp.exp2(s_c - jnp.max(s_c, axis=0, keepdims=True))
    inv_c = jnp.concatenate([any_c] * NSA_HPG, axis=1) / jnp.sum(e_c, axis=0, keepdims=True)
    p_c = e_c * inv_c
    o_cmp = _dot(vct_ref[...], p_c.astype(BF16))
    p_sum = sum(heads(p_c))

    j_row = lax.broadcasted_iota(jnp.int32, (n_sel, 1), 0)
    n_lane = lax.broadcasted_iota(jnp.int32, (1, n_cmp_rows), 1)
    ratio = SEL_BLOCK // CMP_STRIDE
    first_cmp = ratio * j_row - (CMP_LEN - 1) // CMP_STRIDE
    overlap_t = (n_lane >= first_cmp) & (n_lane < ratio * (j_row + 1)) & (n_lane < n_cmp_rows - 1)
    overlap_t = jnp.where(overlap_t, 1.0, 0.0).astype(BF16)
    imp_t = sum(_dot(overlap_t, part) for part in _split3_bf16(p_sum))
    cur = _sel_block_of(t_lane)
    forced = (j_row == 0) | (j_row == cur) | (j_row == cur - 1)
    valid = SEL_BLOCK * j_row <= t_lane
    val = jnp.where(valid, jnp.where(forced, FORCE_SCORE, imp_t), NEG_INF)
    selneg_ref[...] = jnp.where(_top_n_mask(val, valid), 0.0, MASK_OFF)

    def scores_into(s_ref, k0):
        s_ref[...] = _dot(ks_ref[pl.ds(k0, ck), :], qt)

    def accumulate(s_ref, k0, causal_bias):
        j0 = k0 // SEL_BLOCK
        bias = jnp.concatenate(
            [jnp.broadcast_to(selneg_ref[pl.ds(j0 + b, 1), :], (SEL_BLOCK, nq)) for b in range(ck // SEL_BLOCK)],
            axis=0)
        if causal_bias is not None:
            bias = bias + causal_bias
        s = add_per_head(s_ref[...], bias)
        m_run = m_ref[...]
        m_new = jnp.maximum(m_run, jnp.max(s, axis=0, keepdims=True))
        p = jnp.exp2(s - m_new).astype(BF16)
        acc_ref[...] = jnp.exp2(m_run - m_new) * acc_ref[...] + _dot(vst_ref[:, pl.ds(k0, ck)], p)
        m_ref[...] = m_new

    m_ref[...] = jnp.full(m_ref.shape, NEG_INF, F32)
    acc_ref[...] = jnp.zeros(acc_ref.shape, F32)
    scores_into(sa_ref, 0)
    n_past = (c * nq) // ck

    def two_chunks(i2, carry):
        k0 = pl.multiple_of(2 * i2 * ck, 2 * ck)
        scores_into(sb_ref, k0 + ck)
        accumulate(sa_ref, k0, None)
        scores_into(sa_ref, k0 + 2 * ck)
        accumulate(sb_ref, k0 + ck, None)
        return carry

    lax.fori_loop(0, n_past // 2, two_chunks, 0)
    k_last = pl.multiple_of(n_past * ck, ck)
    key_row = lax.broadcasted_iota(jnp.int32, (ck, 1), 0)
    causal = jnp.where(key_row + k_last <= t_lane, 0.0, NEG_INF)

    @pl.when(n_past % 2 == 1)
    def _():
        scores_into(sb_ref, k_last)
        accumulate(sa_ref, k_last - ck, None)
        accumulate(sb_ref, k_last, causal)

    @pl.when(n_past % 2 == 0)
    def _():
        accumulate(sa_ref, k_last, causal)

    o_sel = normalised(acc_ref[...])

    gates_t = jax.nn.sigmoid(gate_ref[...]).T
    outs = [gates_t[3 * h:3 * h + 1] * oc + gates_t[3 * h + 1:3 * h + 2] * osl + gates_t[3 * h + 2:3 * h + 3] * ow
            for h, (oc, osl, ow) in enumerate(zip(heads(o_cmp), heads(o_sel), heads(o_win)))]
    for hp in range(NSA_HPG // 2):
        pair = jnp.concatenate(outs[2 * hp:2 * hp + 2], axis=0)
        o_ref[:, hp * 2 * NSA_DH:(hp + 1) * 2 * NSA_DH] = pair.T.astype(o_ref.dtype)


def _sel_block_of(pos):
    return lax.shift_right_logical(pos, int(math.log2(SEL_BLOCK)))


def _nsa_attn(qt, ks, kw, vst, vwt, kc, vct, zgate, batch):
    t = ks.shape[2]
    m = batch * t
    n_qb = t // Q_BLOCK
    gw = NSA_HPG * NSA_DH
    n_cmp_rows = kc.shape[2]

    def row_map(bb, g, c):
        return (bb * n_qb + c, g)

    def kv_map(bb, g, c):
        return (bb, g, 0, 0)

    def whole(shape):
        return pl.BlockSpec((None, None) + shape, kv_map)

    return pl.pallas_call(
        _nsa_attn_kernel,
        grid=(batch, NSA_KV_GROUPS, n_qb),
        in_specs=[pl.BlockSpec((None, None, AUG_W, NSA_HPG * Q_BLOCK), lambda bb, g, c: (bb, g, 0, c)),
                  whole((t, AUG_W)), whole((WINDOW + t, AUG_W)),
                  whole((V_ROWS, t)), whole((V_ROWS, WINDOW + t)),
                  whole((n_cmp_rows, AUG_W)), whole((NSA_DH, n_cmp_rows)),
                  pl.BlockSpec((Q_BLOCK, 128), row_map)],
        out_specs=pl.BlockSpec((Q_BLOCK, gw), row_map),
        out_shape=jax.ShapeDtypeStruct((m, NSA_KV_GROUPS * gw), BF16),
        scratch_shapes=[pltpu.VMEM((t // SEL_BLOCK, Q_BLOCK), F32),
                        pltpu.VMEM((SEL_KEY_CHUNK, NSA_HPG * Q_BLOCK), F32),
                        pltpu.VMEM((SEL_KEY_CHUNK, NSA_HPG * Q_BLOCK), F32),
                        pltpu.VMEM((1, NSA_HPG * Q_BLOCK), F32),
                        pltpu.VMEM((V_ROWS, NSA_HPG * Q_BLOCK), F32)],
        compiler_params=_cparams("parallel", "parallel", "arbitrary"),
        name="nsa_attn",
    )(qt, ks, kw, vst, vwt, kc, vct, zgate)


def _nsa_q_extras():
    slopes = jnp.asarray(2.0 ** (-8.0 * np.arange(1, NSA_HEADS + 1) / NSA_HEADS) * LOG2E, dtype=F32)
    parts = jnp.stack([p.astype(F32) for p in _split3_bf16(slopes)], axis=1)
    ex = jnp.zeros((NSA_HEADS, AUG_W), F32)
    ex = ex.at[:, AUG_A:AUG_A + 3].set(parts * SEL_BLOCK)
    ex = ex.at[:, AUG_B:AUG_B + 3].set(parts)
    ex = ex.at[:, AUG_N:AUG_N + 3].set(parts * CMP_STRIDE)
    return ex.at[:, AUG_FLAG].set(1.0)


def _merge_kernel(h_ref, a_ref, b_ref, c_ref, zm_ref, wa_ref, wb_ref, wc_ref, wo_ref, o_ref):
    d = h_ref.shape[1]
    gates = jax.nn.sigmoid(zm_ref[...])
    y = (gates[:, :d] * _dot(a_ref[...], wa_ref[...])
         + gates[:, d:2 * d] * _dot(b_ref[...], wb_ref[...])
         + gates[:, 2 * d:] * _dot(c_ref[...], wc_ref[...]))
    o_ref[...] = h_ref[...] + _dot(y.astype(BF16), wo_ref[...])


def _merge(h, a, b, c, zm, wa, wb, wc, wo):
    m, d = h.shape
    tm = ROW_TILE

    def rows(n):
        return pl.BlockSpec((tm, n), lambda i: (i, 0))

    return pl.pallas_call(
        _merge_kernel,
        grid=(m // tm,),
        in_specs=[rows(d), rows(a.shape[1]), rows(b.shape[1]), rows(c.shape[1]), rows(zm.shape[1]),
                  _resident(wa.shape), _resident(wb.shape), _resident(wc.shape), _resident(wo.shape)],
        out_specs=rows(d),
        out_shape=jax.ShapeDtypeStruct((m, d), F32),
        compiler_params=_cparams("parallel"),
        name="merge",
    )(h, a, b, c, zm, wa, wb, wc, wo)


def _xkv_kernel(mem_ref, g_ref, w_ref, kg_ref, k_ref, v_ref):
    mn = _rms(mem_ref[...], g_ref[...]).astype(BF16)
    kv = _dot(mn, w_ref[...])
    dk = k_ref.shape[1]
    for h in range(X_HEADS):
        sl = slice(h * X_DH, (h + 1) * X_DH)
        k_ref[:, sl] = _rms(kv[:, sl], kg_ref[...]).astype(k_ref.dtype)
    v_ref[...] = kv[:, dk:].astype(v_ref.dtype)


def _xkv(mem2, g, w, kg, batch):
    rows, d = mem2.shape
    mlen = rows // batch
    dk = X_HEADS * X_DH
    return pl.pallas_call(
        _xkv_kernel,
        grid=(batch,),
        in_specs=[pl.BlockSpec((mlen, d), lambda bb: (bb, 0)),
                  _resident((1, d)), _resident(w.shape), _resident((1, X_DH))],
        out_specs=[pl.BlockSpec((mlen, dk), lambda bb: (bb, 0)), pl.BlockSpec((mlen, dk), lambda bb: (bb, 0))],
        out_shape=[jax.ShapeDtypeStruct((rows, dk), BF16), jax.ShapeDtypeStruct((rows, dk), BF16)],
        compiler_params=_cparams("parallel"),
        name="xattn_kv",
    )(mem2, g.reshape(1, d), w, kg.reshape(1, X_DH))


def _xattn_kernel(h_ref, g_ref, wq_ref, qg_ref, k_ref, v_ref, wo_ref, o_ref):
    h = h_ref[...]
    q = _dot(_rms(h, g_ref[...]).astype(BF16), wq_ref[...])
    scale = 1.0 / math.sqrt(X_DH)
    outs = []
    for hd in range(X_HEADS):
        sl = slice(hd * X_DH, (hd + 1) * X_DH)
        qh = _rms(q[:, sl], qg_ref[...]).astype(BF16)
        s = _dot_nt(qh, k_ref[:, sl]) * scale
        e = jnp.exp(s - jnp.max(s, axis=-1, keepdims=True))
        p = e / jnp.sum(e, axis=-1, keepdims=True)
        outs.append(_dot(p.astype(BF16), v_ref[:, sl]))
    o = jnp.concatenate(outs, axis=-1).astype(BF16)
    o_ref[...] = h + _dot(o, wo_ref[...])


def _xattn(h, g, wq, qg, k, v, wo, batch):
    m, d = h.shape
    tm = ROW_TILE
    per_b = m // batch // tm
    mlen = k.shape[0] // batch
    dk = k.shape[1]
    return pl.pallas_call(
        _xattn_kernel,
        grid=(batch, per_b),
        in_specs=[pl.BlockSpec((tm, d), lambda bb, i: (bb * per_b + i, 0)),
                  _resident((1, d)), _resident(wq.shape), _resident((1, X_DH)),
                  pl.BlockSpec((mlen, dk), lambda bb, i: (bb, 0)),
                  pl.BlockSpec((mlen, dk), lambda bb, i: (bb, 0)),
                  _resident(wo.shape)],
        out_specs=pl.BlockSpec((tm, d), lambda bb, i: (bb * per_b + i, 0)),
        out_shape=jax.ShapeDtypeStruct((m, d), F32),
        compiler_params=_cparams("parallel", "parallel"),
        name="xattn",
    )(h, g.reshape(1, d), wq, qg.reshape(1, X_DH), k, v, wo)


def _inproj_layout(d_conv, d_sgu):
    cols_q = NSA_HEADS * NSA_DH
    kv0 = 2 * d_conv + 2 * d_sgu + cols_q
    gdh = NSA_KV_GROUPS * NSA_DH
    gate0 = kv0 + 6 * gdh
    merge0 = gate0 + NSA_HEADS * 3
    d = np.arange(NSA_DH)
    kvs = np.concatenate([kv0 + i * gdh + g * NSA_DH + d
                          for g in range(NSA_KV_GROUPS) for i in (2, 4, 3, 5)])
    cmp = kv0 + np.arange(2 * gdh)
    return kv0, kvs, cmp, gate0, merge0


def _prep_w_in(w_in, d_conv, d_sgu):
    kv0, kvs, cmp, gate0, merge0 = _inproj_layout(d_conv, d_sgu)
    per_g = NSA_HPG * 3
    gate_blocks = []
    for g in range(NSA_KV_GROUPS):
        blk = w_in[:, :, gate0 + g * per_g:gate0 + (g + 1) * per_g]
        gate_blocks.append(jnp.pad(blk, ((0, 0), (0, 0), (0, 128 - per_g))))
    w_all = jnp.concatenate([w_in[:, :, :kv0], w_in[:, :, kvs], w_in[:, :, cmp]] + gate_blocks
                            + [w_in[:, :, merge0:]], axis=-1)
    d_model = w_in.shape[1]
    widths = (2 * d_conv, 2 * d_sgu, NSA_HEADS * NSA_DH, len(kvs), len(cmp), 128 * NSA_KV_GROUPS, 3 * d_model)
    return w_all.astype(BF16), widths


def kernel(x, mem, norm_ffn1, ffn1_wi, ffn1_wo, norm_mix, w_in, conv_w, conv_b, conv_ln_g, conv_ln_b, conv_out,
           sgu_norm, sgu_ws, sgu_b, sgu_out, nsa_q_norm, nsa_k_norm, cmp_pe, cmp_w1, cmp_w2, nsa_out, w_out,
           norm_xattn, mem_norm, xq, xkv, xq_norm, xk_norm, xo, norm_ffn2, ffn2_wi, ffn2_wo):
    batch, t, d_model = x.shape
    depth = w_in.shape[0]
    d_conv = conv_w.shape[-1]
    d_sgu = sgu_norm.shape[-1]
    m = batch * t

    w_all, widths = _prep_w_in(w_in, d_conv, d_sgu)
    bf = lambda w: w.astype(BF16)
    ffn1_wi, ffn1_wo, ffn2_wi, ffn2_wo = bf(ffn1_wi), bf(ffn1_wo), bf(ffn2_wi), bf(ffn2_wo)
    conv_out, sgu_out, nsa_out, w_out = bf(conv_out), bf(sgu_out), bf(nsa_out), bf(w_out)
    xq, xkv, xo = bf(xq), bf(xkv), bf(xo)
    sgu_b_exp = jnp.repeat(jnp.swapaxes(sgu_b, 1, 2), d_sgu // SGU_GROUPS, axis=2)
    q_extras = _nsa_q_extras()
    pe_flat = cmp_pe.reshape(depth, 2, CMP_LEN * NSA_DH)

    h = x.reshape(m, d_model)
    mem2 = mem.reshape(-1, d_model)
    for l in range(depth):
        h = _ffn(h, norm_ffn1[l], ffn1_wi[l], ffn1_wo[l])

        z_conv, z_sgu, z_q, z_kvs, z_cmp, z_gate, z_merge = _inproj(h, norm_mix[l], w_all[l], widths)
        act_a = _conv(z_conv, conv_w[l], conv_b[l], conv_ln_g[l], conv_ln_b[l], batch)
        act_b = _sgu(z_sgu, sgu_norm[l], sgu_ws[l], sgu_b_exp[l])

        qg2 = jnp.tile(nsa_q_norm[l], 2).reshape(1, 2 * NSA_DH)
        kg2 = jnp.concatenate([nsa_k_norm[l, 1], nsa_k_norm[l, 2]]).reshape(1, 2 * NSA_DH)
        qt, ks, kw, vst, vwt = _nsa_prep(z_q, z_kvs, qg2, kg2, q_extras, batch)
        x16 = (z_cmp.reshape(batch, t // CMP_STRIDE, CMP_STRIDE, 2, NSA_KV_GROUPS, NSA_DH)
               .transpose(0, 3, 4, 1, 2, 5)
               .reshape(batch, 2, NSA_KV_GROUPS, t // CMP_STRIDE, CMP_STRIDE * NSA_DH))
        kc, vct = _nsa_cmp(x16, pe_flat[l], cmp_w1[l], cmp_w2[l], nsa_k_norm[l, 0])
        act_c = _nsa_attn(qt, ks, kw, vst, vwt, kc, vct, z_gate, batch)

        h = _merge(h, act_a, act_b, act_c, z_merge, conv_out[l], sgu_out[l], nsa_out[l], w_out[l])

        xk, xv = _xkv(mem2, mem_norm[l], xkv[l], xk_norm[l], batch)
        h = _xattn(h, norm_xattn[l], xq[l], xq_norm[l], xk, xv, xo[l], batch)

        h = _ffn(h, norm_ffn2[l], ffn2_wi[l], ffn2_wo[l])
    return h.reshape(batch, t, d_model)
```

```python
import functools
import math

import numpy as np
import jax
import jax.numpy as jnp
from jax import lax
from jax.experimental import pallas as pl
from jax.experimental.pallas import tpu as pltpu

F32 = jnp.float32
BF16 = jnp.bfloat16

EPS = 1e-6
NEG_INF = -1e30
FORCE_SCORE = 1e4
LOG2E = math.log2(math.e)

CONV_WIDTH = 31
CONV_HALO = 32
SGU_CHUNK = 128
SGU_GROUPS = 4
NSA_HEADS = 8
NSA_KV_GROUPS = 2
NSA_HPG = NSA_HEADS // NSA_KV_GROUPS
NSA_DH = 64
CMP_LEN = 32
CMP_STRIDE = 16
SEL_BLOCK = 64
TOP_N = 16
WINDOW = 512
Q_BLOCK = 128
X_HEADS = 4
X_DH = 128

AUG_W = 128
AUG_A = NSA_DH
AUG_B = NSA_DH + 3
AUG_N = NSA_DH + 6
AUG_FLAG = NSA_DH + 9
MASK_OFF = -(2.0 ** 100)
V_ROWS = NSA_DH + 16

ROW_TILE = 512
INPROJ_ROW_TILE = 256
SEL_KEY_CHUNK = 256
VMEM_LIMIT = 56 * 1024 * 1024


def _cparams(*sem):
    return pltpu.CompilerParams(dimension_semantics=sem, vmem_limit_bytes=VMEM_LIMIT)


def _resident(shape):
    nd = len(shape)
    return pl.BlockSpec(shape, lambda *_: (0,) * nd, pipeline_mode=pl.Buffered(1))


def _rms(x, g):
    return x * lax.rsqrt(jnp.mean(x * x, axis=-1, keepdims=True) + EPS) * g


def _dot(a, b):
    return jnp.dot(a, b, preferred_element_type=F32)


def _dot_nt(a, b):
    return lax.dot_general(a, b, (((1,), (1,)), ((), ())), preferred_element_type=F32)


def _ffn_kernel(x_ref, g_ref, wi_ref, wo_ref, o_ref, *, d_ff, n_chunks):
    x = x_ref[...]
    xn = _rms(x, g_ref[...]).astype(BF16)
    ck = d_ff // n_chunks
    acc = x
    for c in range(n_chunks):
        a = _dot(xn, wi_ref[:, c * ck:(c + 1) * ck])
        b = _dot(xn, wi_ref[:, d_ff + c * ck:d_ff + (c + 1) * ck])
        mid = (a * jax.nn.sigmoid(a) * b).astype(BF16)
        acc = acc + 0.5 * _dot(mid, wo_ref[c * ck:(c + 1) * ck, :])
    o_ref[...] = acc


def _ffn(h, g, wi, wo):
    m, d = h.shape
    d_ff = wo.shape[0]
    tm = ROW_TILE
    return pl.pallas_call(
        functools.partial(_ffn_kernel, d_ff=d_ff, n_chunks=2),
        grid=(m // tm,),
        in_specs=[pl.BlockSpec((tm, d), lambda i: (i, 0)),
                  _resident((1, d)), _resident(wi.shape), _resident(wo.shape)],
        out_specs=pl.BlockSpec((tm, d), lambda i: (i, 0)),
        out_shape=jax.ShapeDtypeStruct((m, d), F32),
        compiler_params=_cparams("parallel"),
        name="ffn",
    )(h, g.reshape(1, d), wi, wo)


def _inproj_kernel(x_ref, g_ref, w_ref, *o_refs):
    xn = _rms(x_ref[...], g_ref[...]).astype(BF16)
    off = 0
    for o_ref in o_refs:
        n = o_ref.shape[1]
        o_ref[...] = _dot(xn, w_ref[:, off:off + n]).astype(o_ref.dtype)
        off += n


def _inproj(h, g, w, widths):
    m, d = h.shape
    tm = INPROJ_ROW_TILE
    return pl.pallas_call(
        _inproj_kernel,
        grid=(m // tm,),
        in_specs=[pl.BlockSpec((tm, d), lambda i: (i, 0)), _resident((1, d)), _resident(w.shape)],
        out_specs=[pl.BlockSpec((tm, n), lambda i: (i, 0)) for n in widths],
        out_shape=[jax.ShapeDtypeStruct((m, n), F32) for n in widths],
        compiler_params=_cparams("parallel"),
        name="inproj",
    )(h, g.reshape(1, d), w)


def _conv_kernel(cur_ref, prev_ref, w_ref, b_ref, lg_ref, lb_ref, o_ref, buf_ref, *, d_conv):
    i = pl.program_id(1)
    tt = cur_ref.shape[0]
    cur = cur_ref[...]
    prev = prev_ref[...]
    keep = (i > 0).astype(F32)
    buf_ref[0:CONV_HALO, :] = prev[:, :d_conv] * jax.nn.sigmoid(prev[:, d_conv:]) * keep
    buf_ref[CONV_HALO:, :] = cur[:, :d_conv] * jax.nn.sigmoid(cur[:, d_conv:])
    base = CONV_HALO - (CONV_WIDTH - 1)
    acc = jnp.zeros((tt, d_conv), F32)
    for k in range(CONV_WIDTH):
        acc = acc + w_ref[k:k + 1, :] * buf_ref[base + k:base + k + tt, :]
    y = acc + b_ref[...]
    mu = jnp.mean(y, axis=-1, keepdims=True)
    yc = y - mu
    var = jnp.mean(yc * yc, axis=-1, keepdims=True)
    yn = yc * lax.rsqrt(var + EPS) * lg_ref[...] + lb_ref[...]
    o_ref[...] = (yn * jax.nn.sigmoid(yn)).astype(o_ref.dtype)


def _conv(z, w, b, lg, lb, batch):
    m, two_d = z.shape
    d_conv = two_d // 2
    t = m // batch
    tt = ROW_TILE
    per_b = t // tt
    halo_per_tile = tt // CONV_HALO
    return pl.pallas_call(
        functools.partial(_conv_kernel, d_conv=d_conv),
        grid=(batch, per_b),
        in_specs=[pl.BlockSpec((tt, two_d), lambda bb, i: (bb * per_b + i, 0)),
                  pl.BlockSpec((CONV_HALO, two_d),
                               lambda bb, i: (jnp.maximum((bb * per_b + i) * halo_per_tile - 1, 0), 0)),
                  _resident(w.shape), _resident((1, d_conv)), _resident((1, d_conv)), _resident((1, d_conv))],
        out_specs=pl.BlockSpec((tt, d_conv), lambda bb, i: (bb * per_b + i, 0)),
        out_shape=jax.ShapeDtypeStruct((m, d_conv), BF16),
        scratch_shapes=[pltpu.VMEM((CONV_HALO + tt, d_conv), F32)],
        compiler_params=_cparams("parallel", "parallel"),
        name="conv",
    )(z, z, w, b.reshape(1, -1), lg.reshape(1, -1), lb.reshape(1, -1))


def _sgu_kernel(z_ref, g_ref, ws_ref, b_ref, o_ref, *, d_sgu):
    tm = z_ref.shape[0]
    gd = d_sgu // SGU_GROUPS
    ge = jax.nn.gelu(z_ref[...])
    u = ge[:, :d_sgu]
    vn = _rms(ge[:, d_sgu:], g_ref[...]).astype(BF16)
    row = lax.broadcasted_iota(jnp.int32, (SGU_CHUNK, SGU_CHUNK), 0)
    col = lax.broadcasted_iota(jnp.int32, (SGU_CHUNK, SGU_CHUNK), 1)
    for gr in range(SGU_GROUPS):
        wt = jnp.where(row >= col, ws_ref[gr], 0.0).astype(BF16)
        for ch in range(tm // SGU_CHUNK):
            rows = slice(ch * SGU_CHUNK, (ch + 1) * SGU_CHUNK)
            cols = slice(gr * gd, (gr + 1) * gd)
            mixed = _dot(wt, vn[rows, cols]) + b_ref[:, cols]
            o_ref[rows, cols] = (u[rows, cols] * mixed).astype(o_ref.dtype)


def _sgu(z, g, ws, b_exp):
    m, two_d = z.shape
    d_sgu = two_d // 2
    tm = ROW_TILE
    return pl.pallas_call(
        functools.partial(_sgu_kernel, d_sgu=d_sgu),
        grid=(m // tm,),
        in_specs=[pl.BlockSpec((tm, two_d), lambda i: (i, 0)),
                  _resident((1, d_sgu)), _resident(ws.shape), _resident(b_exp.shape)],
        out_specs=pl.BlockSpec((tm, d_sgu), lambda i: (i, 0)),
        out_shape=jax.ShapeDtypeStruct((m, d_sgu), BF16),
        compiler_params=_cparams("parallel"),
        name="sgu",
    )(z, g.reshape(1, -1), ws, b_exp)


def _half_rms(x, gains):
    lane = lax.broadcasted_iota(jnp.int32, x.shape, 1)
    lo = lane < NSA_DH
    x2 = x * x
    s_lo = jnp.sum(jnp.where(lo, x2, 0.0), axis=-1, keepdims=True)
    s_hi = jnp.sum(jnp.where(lo, 0.0, x2), axis=-1, keepdims=True)
    inv = lax.rsqrt(jnp.where(lo, s_lo, s_hi) * (1.0 / NSA_DH) + EPS)
    return x * inv * gains


def _nsa_prep_kernel(q_ref, kv_ref, qg_ref, kg_ref, ex_ref, qt_ref, ks_ref, kw_ref, vst_ref, vwt_ref):
    g = pl.program_id(1)
    i = pl.program_id(2)
    tp = q_ref.shape[0]
    pair_w = 2 * NSA_DH
    lane = lax.broadcasted_iota(jnp.int32, (tp, AUG_W), 1)
    head_lanes = lane < NSA_DH
    is_pad = i == 0

    q = q_ref[...]
    q_scale = LOG2E / math.sqrt(NSA_DH)
    for hp in range(NSA_HPG // 2):
        pair = _half_rms(q[:, hp * pair_w:(hp + 1) * pair_w], qg_ref[...]) * q_scale
        for sub in range(2):
            h = 2 * hp + sub
            x = pair if sub == 0 else pltpu.roll(pair, NSA_DH, 1)
            qa = jnp.where(head_lanes, x, ex_ref[pl.ds(g * NSA_HPG + h, 1), :])
            qat = qa.T.astype(qt_ref.dtype)
            for cb in range(tp // Q_BLOCK):
                col = (cb * NSA_HPG + h) * Q_BLOCK
                qt_ref[:, col:col + Q_BLOCK] = qat[:, cb * Q_BLOCK:(cb + 1) * Q_BLOCK]

    kv = kv_ref[...]
    kpair = _half_rms(kv[:, :pair_w], kg_ref[...])
    pos = (i - 1) * tp + lax.broadcasted_iota(jnp.int32, (tp, 1), 0)
    blk = lax.shift_right_arithmetic(pos, int(math.log2(SEL_BLOCK))).astype(F32)
    off = (pos & (SEL_BLOCK - 1)).astype(F32)
    aug = jnp.where((lane >= AUG_A) & (lane < AUG_A + 3), blk,
                    jnp.where((lane >= AUG_B) & (lane < AUG_B + 3), off, 0.0))
    pad_key = jnp.where(lane == AUG_FLAG, MASK_OFF, 0.0)
    ks_ref[...] = jnp.where(head_lanes, kpair, aug).astype(ks_ref.dtype)
    kw = jnp.where(head_lanes, pltpu.roll(kpair, NSA_DH, 1), aug)
    kw_ref[...] = jnp.where(is_pad, pad_key, kw).astype(kw_ref.dtype)

    vpair = kv[:, pair_w:]
    ones_lane = jnp.where(lane == NSA_DH, 1.0, 0.0)
    vst_ref[...] = jnp.where(head_lanes, vpair, ones_lane).T[:V_ROWS].astype(vst_ref.dtype)
    vw = jnp.where(head_lanes, pltpu.roll(vpair, NSA_DH, 1), ones_lane)
    vwt_ref[...] = jnp.where(is_pad, 0.0, vw).T[:V_ROWS].astype(vwt_ref.dtype)


def _nsa_prep(zq, zkvs, qg2, kg2, q_extras, batch):
    m = zq.shape[0]
    t = m // batch
    tp = WINDOW
    per_b = t // tp
    gw = NSA_HPG * NSA_DH
    g_n = NSA_KV_GROUPS

    def in_map(bb, g, i):
        return (bb * per_b + jnp.maximum(i - 1, 0), g)

    def rows_now(bb, g, i):
        return (bb, g, jnp.maximum(i - 1, 0), 0)

    def cols_now(bb, g, i):
        return (bb, g, 0, jnp.maximum(i - 1, 0))

    return pl.pallas_call(
        _nsa_prep_kernel,
        grid=(batch, g_n, per_b + 1),
        in_specs=[pl.BlockSpec((tp, gw), in_map), pl.BlockSpec((tp, gw), in_map),
                  _resident((1, 2 * NSA_DH)), _resident((1, 2 * NSA_DH)), _resident(q_extras.shape)],
        out_specs=[pl.BlockSpec((None, None, AUG_W, tp * NSA_HPG), cols_now),
                   pl.BlockSpec((None, None, tp, AUG_W), rows_now),
                   pl.BlockSpec((None, None, tp, AUG_W), lambda bb, g, i: (bb, g, i, 0)),
                   pl.BlockSpec((None, None, V_ROWS, tp), cols_now),
                   pl.BlockSpec((None, None, V_ROWS, tp), lambda bb, g, i: (bb, g, 0, i))],
        out_shape=[jax.ShapeDtypeStruct((batch, g_n, AUG_W, t * NSA_HPG), BF16),
                   jax.ShapeDtypeStruct((batch, g_n, t, AUG_W), BF16),
                   jax.ShapeDtypeStruct((batch, g_n, WINDOW + t, AUG_W), BF16),
                   jax.ShapeDtypeStruct((batch, g_n, V_ROWS, t), BF16),
                   jax.ShapeDtypeStruct((batch, g_n, V_ROWS, WINDOW + t), BF16)],
        compiler_params=_cparams("parallel", "parallel", "arbitrary"),
        name="nsa_prep",
    )(zq, zkvs, qg2, kg2, q_extras)


def _nsa_cmp_kernel(xk_ref, xv_ref, pe_ref, w1_ref, w2_ref, kg_ref, kc_ref, vct_ref):
    half = CMP_STRIDE * NSA_DH
    rows = xk_ref.shape[0]
    rid = lax.broadcasted_iota(jnp.int32, (rows, 1), 0)

    def mlp(x_ref, kind):
        x = x_ref[...].astype(BF16)
        w1 = w1_ref[kind].astype(BF16)
        first = _dot(x, w1[:half])
        second = _dot(x, w1[half:])
        second = jnp.where(rid < rows - 1, pltpu.roll(second, rows - 1, 0), 0.0)
        pe = jnp.broadcast_to(pe_ref[kind:kind + 1, :], (8, 2 * half)).astype(BF16)
        hid = first + second + _dot(pe, w1)[0:1]
        hid = (hid * jax.nn.sigmoid(hid)).astype(BF16)
        return _dot(hid, w2_ref[kind].astype(BF16))

    lane = lax.broadcasted_iota(jnp.int32, (rows, AUG_W), 1)
    head_lanes = lane < NSA_DH
    zeros = jnp.zeros((rows, AUG_W - NSA_DH), F32)
    kc = jnp.concatenate([_rms(mlp(xk_ref, 0), kg_ref[...]), zeros], axis=1)
    aug = jnp.where((lane >= AUG_N) & (lane < AUG_N + 3), rid.astype(F32), 0.0)
    kc_ref[...] = jnp.where(head_lanes, kc, aug).astype(kc_ref.dtype)
    vc = jnp.concatenate([mlp(xv_ref, 1), zeros], axis=1)
    vct_ref[...] = vc.T[:NSA_DH].astype(vct_ref.dtype)


def _nsa_cmp(x16, pe, w1, w2, kg0):
    batch, _, g, rows, width = x16.shape
    return pl.pallas_call(
        _nsa_cmp_kernel,
        grid=(batch, g),
        in_specs=[pl.BlockSpec((None, None, None, rows, width), lambda bb, gg: (bb, 0, gg, 0, 0)),
                  pl.BlockSpec((None, None, None, rows, width), lambda bb, gg: (bb, 1, gg, 0, 0)),
                  _resident(pe.shape), _resident(w1.shape), _resident(w2.shape), _resident((1, NSA_DH))],
        out_specs=[pl.BlockSpec((None, None, rows, AUG_W), lambda bb, gg: (bb, gg, 0, 0)),
                   pl.BlockSpec((None, None, NSA_DH, rows), lambda bb, gg: (bb, gg, 0, 0))],
        out_shape=[jax.ShapeDtypeStruct((batch, g, rows, AUG_W), BF16),
                   jax.ShapeDtypeStruct((batch, g, NSA_DH, rows), BF16)],
        compiler_params=_cparams("parallel", "parallel"),
        name="nsa_cmp",
    )(x16, x16, pe, w1, w2, kg0.reshape(1, NSA_DH))


def _split3_bf16(x):
    hi = x.astype(BF16)
    r1 = x - hi.astype(F32)
    mid = r1.astype(BF16)
    lo = (r1 - mid.astype(F32)).astype(BF16)
    return hi, mid, lo


def _top_n_mask(val, valid):
    n_sel = val.shape[0]
    groups = [val[8 * k:8 * k + 8] for k in range(n_sel // 8)]
    ranks = [jnp.zeros(g.shape, F32) for g in groups]
    sub = lax.broadcasted_iota(jnp.int32, groups[0].shape, 0)
    for i in range(n_sel):
        vi = jnp.broadcast_to(val[i:i + 1, :], groups[0].shape)
        for k, vk in enumerate(groups):
            if 8 * k > i:
                ahead = vi >= vk
            elif 8 * k + 7 < i:
                ahead = vi > vk
            else:
                ahead = (vi > vk) | ((vi == vk) & (sub + 8 * k > i))
            ranks[k] = ranks[k] + jnp.where(ahead, 1.0, 0.0)
    rank = jnp.concatenate(ranks, axis=0)
    return (rank < TOP_N) & valid


def _nsa_attn_kernel(qt_ref, ks_ref, kw_ref, vst_ref, vwt_ref, kc_ref, vct_ref, gate_ref, o_ref,
                     selneg_ref, sa_ref, sb_ref, pa_ref, pb_ref, m_ref, acc_ref):
    c = pl.program_id(2)
    nq = Q_BLOCK
    n_sel = ks_ref.shape[0] // SEL_BLOCK
    n_cmp_rows = kc_ref.shape[0]
    ck = SEL_KEY_CHUNK
    wk = WINDOW + nq
    qt = qt_ref[...]
    q_lane = lax.broadcasted_iota(jnp.int32, (1, nq), 1)
    t_lane = c * nq + q_lane

    def heads(x):
        return [x[:, h * nq:(h + 1) * nq] for h in range(NSA_HPG)]

    def add_per_head(x, bias):
        return jnp.concatenate([xh + bias for xh in heads(x)], axis=1)

    def normalised(acc):
        return acc[:NSA_DH] * (1.0 / acc[NSA_DH:NSA_DH + 1])

    w0 = pl.multiple_of(c * nq, nq)
    e_row = lax.broadcasted_iota(jnp.int32, (nq, 1), 0)
    s_c_raw = _dot(kc_ref[...], qt)
    s_w = _dot(kw_ref[pl.ds(w0, wk), :], qt)
    sa_ref[...] = _dot(ks_ref[0:ck, :], qt)

    def window_output():
        s = jnp.concatenate([add_per_head(s_w[:nq], jnp.where(e_row > q_lane, 0.0, NEG_INF)),
                             s_w[nq:WINDOW],
                             add_per_head(s_w[WINDOW:], jnp.where(e_row <= q_lane, 0.0, NEG_INF))], axis=0)
        p = jnp.exp2(s - jnp.max(s, axis=0, keepdims=True)).astype(BF16)
        return normalised(_dot(vwt_ref[:, pl.ds(w0, wk)], p))

    n_row = lax.broadcasted_iota(jnp.int32, (n_cmp_rows, 1), 0)
    ok_c = CMP_STRIDE * n_row + (CMP_LEN - 1) <= t_lane
    any_c = jnp.where(t_lane >= CMP_LEN - 1, 1.0, 0.0)
    s_c = add_per_head(s_c_raw, jnp.where(ok_c, 0.0, NEG_INF))
    e_c = jnp.exp2(s_c - jnp.max(s_c, axis=0, keepdims=True))
    inv_c = jnp.concatenate([any_c] * NSA_HPG, axis=1) / jnp.sum(e_c, axis=0, keepdims=True)
    p_c = e_c * inv_c
    o_cmp = _dot(vct_ref[...], p_c.astype(BF16))
    p_sum = sum(heads(p_c))

    j_row = lax.broadcasted_iota(jnp.int32, (n_sel, 1), 0)
    n_lane = lax.broadcasted_iota(jnp.int32, (1, n_cmp_rows), 1)
    ratio = SEL_BLOCK // CMP_STRIDE
    first_cmp = ratio * j_row - (CMP_LEN - 1) // CMP_STRIDE
    overlap_t = (n_lane >= first_cmp) & (n_lane < ratio * (j_row + 1)) & (n_lane < n_cmp_rows - 1)
    overlap_t = jnp.where(overlap_t, 1.0, 0.0).astype(BF16)
    imp_t = sum(_dot(overlap_t, part) for part in _split3_bf16(p_sum))
    cur = _sel_block_of(t_lane)
    forced = (j_row == 0) | (j_row == cur) | (j_row == cur - 1)
    valid = SEL_BLOCK * j_row <= t_lane
    val = jnp.where(valid, jnp.where(forced, FORCE_SCORE, imp_t), NEG_INF)
    selneg_ref[...] = jnp.where(_top_n_mask(val, valid), 0.0, MASK_OFF)
    o_win = window_output()

    def scores_into(s_ref, k0):
        s_ref[...] = _dot(ks_ref[pl.ds(pl.multiple_of(k0, ck), ck), :], qt)

    def softmax_into(s_ref, p_ref, k0, causal_bias):
        j0 = k0 // SEL_BLOCK
        bias = jnp.concatenate(
            [jnp.broadcast_to(selneg_ref[pl.ds(j0 + b, 1), :], (SEL_BLOCK, nq)) for b in range(ck // SEL_BLOCK)],
            axis=0)
        if causal_bias is not None:
            bias = bias + causal_bias
        s = add_per_head(s_ref[...], bias)
        m_run = m_ref[...]
        m_new = jnp.maximum(m_run, jnp.max(s, axis=0, keepdims=True))
        p_ref[...] = jnp.exp2(s - m_new).astype(p_ref.dtype)
        m_ref[...] = m_new
        return jnp.exp2(m_run - m_new)

    def weighted_values(p_ref, k0):
        return _dot(vst_ref[:, pl.ds(pl.multiple_of(k0, ck), ck)], p_ref[...])

    def fold(alpha, pending):
        acc_ref[...] = alpha * (acc_ref[...] + pending)

    m_ref[...] = jnp.full(m_ref.shape, NEG_INF, F32)
    acc_ref[...] = jnp.zeros(acc_ref.shape, F32)
    pb_ref[...] = jnp.zeros(pb_ref.shape, pb_ref.dtype)
    n_past = (c * nq) // ck

    def two_chunks(i2, carry):
        k0 = pl.multiple_of(2 * i2 * ck, 2 * ck)
        pending = weighted_values(pb_ref, jnp.maximum(k0 - ck, 0))
        scores_into(sb_ref, k0 + ck)
        fold(softmax_into(sa_ref, pa_ref, k0, None), pending)
        pending = weighted_values(pa_ref, k0)
        scores_into(sa_ref, k0 + 2 * ck)
        fold(softmax_into(sb_ref, pb_ref, k0 + ck, None), pending)
        return carry

    lax.fori_loop(0, n_past // 2, two_chunks, 0)
    k_last = pl.multiple_of(n_past * ck, ck)
    key_row = lax.broadcasted_iota(jnp.int32, (ck, 1), 0)
    causal = jnp.where(key_row + k_last <= t_lane, 0.0, NEG_INF)

    @pl.when(n_past % 2 == 1)
    def _():
        pending = weighted_values(pb_ref, jnp.maximum(k_last - 2 * ck, 0))
        scores_into(sb_ref, k_last)
        fold(softmax_into(sa_ref, pa_ref, k_last - ck, None), pending)
        pending = weighted_values(pa_ref, k_last - ck)
        fold(softmax_into(sb_ref, pb_ref, k_last, causal), pending)
        acc_ref[...] = acc_ref[...] + weighted_values(pb_ref, k_last)

    @pl.when(n_past % 2 == 0)
    def _():
        pending = weighted_values(pb_ref, jnp.maximum(k_last - ck, 0))
        fold(softmax_into(sa_ref, pa_ref, k_last, causal), pending)
        acc_ref[...] = acc_ref[...] + weighted_values(pa_ref, k_last)

    o_sel = normalised(acc_ref[...])

    gates_t = jax.nn.sigmoid(gate_ref[...]).T
    outs = [gates_t[3 * h:3 * h + 1] * oc + gates_t[3 * h + 1:3 * h + 2] * osl + gates_t[3 * h + 2:3 * h + 3] * ow
            for h, (oc, osl, ow) in enumerate(zip(heads(o_cmp), heads(o_sel), heads(o_win)))]
    for hp in range(NSA_HPG // 2):
        pair = jnp.concatenate(outs[2 * hp:2 * hp + 2], axis=0)
        o_ref[:, hp * 2 * NSA_DH:(hp + 1) * 2 * NSA_DH] = pair.T.astype(o_ref.dtype)


def _sel_block_of(pos):
    return lax.shift_right_logical(pos, int(math.log2(SEL_BLOCK)))


def _nsa_attn(qt, ks, kw, vst, vwt, kc, vct, zgate, batch):
    t = ks.shape[2]
    m = batch * t
    n_qb = t // Q_BLOCK
    gw = NSA_HPG * NSA_DH
    n_cmp_rows = kc.shape[2]

    def row_map(bb, g, c):
        return (bb * n_qb + c, g)

    def kv_map(bb, g, c):
        return (bb, g, 0, 0)

    def whole(shape):
        return pl.BlockSpec((None, None) + shape, kv_map)

    return pl.pallas_call(
        _nsa_attn_kernel,
        grid=(batch, NSA_KV_GROUPS, n_qb),
        in_specs=[pl.BlockSpec((None, None, AUG_W, NSA_HPG * Q_BLOCK), lambda bb, g, c: (bb, g, 0, c)),
                  whole((t, AUG_W)), whole((WINDOW + t, AUG_W)),
                  whole((V_ROWS, t)), whole((V_ROWS, WINDOW + t)),
                  whole((n_cmp_rows, AUG_W)), whole((NSA_DH, n_cmp_rows)),
                  pl.BlockSpec((Q_BLOCK, 128), row_map)],
        out_specs=pl.BlockSpec((Q_BLOCK, gw), row_map),
        out_shape=jax.ShapeDtypeStruct((m, NSA_KV_GROUPS * gw), BF16),
        scratch_shapes=[pltpu.VMEM((t // SEL_BLOCK, Q_BLOCK), F32),
                        pltpu.VMEM((SEL_KEY_CHUNK, NSA_HPG * Q_BLOCK), F32),
                        pltpu.VMEM((SEL_KEY_CHUNK, NSA_HPG * Q_BLOCK), F32),
                        pltpu.VMEM((SEL_KEY_CHUNK, NSA_HPG * Q_BLOCK), BF16),
                        pltpu.VMEM((SEL_KEY_CHUNK, NSA_HPG * Q_BLOCK), BF16),
                        pltpu.VMEM((1, NSA_HPG * Q_BLOCK), F32),
                        pltpu.VMEM((V_ROWS, NSA_HPG * Q_BLOCK), F32)],
        compiler_params=_cparams("parallel", "parallel", "arbitrary"),
        name="nsa_attn",
    )(qt, ks, kw, vst, vwt, kc, vct, zgate)


def _nsa_q_extras():
    slopes = jnp.asarray(2.0 ** (-8.0 * np.arange(1, NSA_HEADS + 1) / NSA_HEADS) * LOG2E, dtype=F32)
    parts = jnp.stack([p.astype(F32) for p in _split3_bf16(slopes)], axis=1)
    ex = jnp.zeros((NSA_HEADS, AUG_W), F32)
    ex = ex.at[:, AUG_A:AUG_A + 3].set(parts * SEL_BLOCK)
    ex = ex.at[:, AUG_B:AUG_B + 3].set(parts)
    ex = ex.at[:, AUG_N:AUG_N + 3].set(parts * CMP_STRIDE)
    return ex.at[:, AUG_FLAG].set(1.0)


def _merge_kernel(h_ref, a_ref, b_ref, c_ref, zm_ref, wa_ref, wb_ref, wc_ref, wo_ref, o_ref):
    d = h_ref.shape[1]
    gates = jax.nn.sigmoid(zm_ref[...])
    y = (gates[:, :d] * _dot(a_ref[...], wa_ref[...])
         + gates[:, d:2 * d] * _dot(b_ref[...], wb_ref[...])
         + gates[:, 2 * d:] * _dot(c_ref[...], wc_ref[...]))
    o_ref[...] = h_ref[...] + _dot(y.astype(BF16), wo_ref[...])


def _merge(h, a, b, c, zm, wa, wb, wc, wo):
    m, d = h.shape
    tm = ROW_TILE

    def rows(n):
        return pl.BlockSpec((tm, n), lambda i: (i, 0))

    return pl.pallas_call(
        _merge_kernel,
        grid=(m // tm,),
        in_specs=[rows(d), rows(a.shape[1]), rows(b.shape[1]), rows(c.shape[1]), rows(zm.shape[1]),
                  _resident(wa.shape), _resident(wb.shape), _resident(wc.shape), _resident(wo.shape)],
        out_specs=rows(d),
        out_shape=jax.ShapeDtypeStruct((m, d), F32),
        compiler_params=_cparams("parallel"),
        name="merge",
    )(h, a, b, c, zm, wa, wb, wc, wo)


def _xkv_kernel(mem_ref, g_ref, w_ref, kg_ref, k_ref, v_ref):
    mn = _rms(mem_ref[...], g_ref[...]).astype(BF16)
    kv = _dot(mn, w_ref[...])
    dk = k_ref.shape[1]
    for h in range(X_HEADS):
        sl = slice(h * X_DH, (h + 1) * X_DH)
        k_ref[:, sl] = _rms(kv[:, sl], kg_ref[...]).astype(k_ref.dtype)
    v_ref[...] = kv[:, dk:].astype(v_ref.dtype)


def _xkv(mem2, g, w, kg, batch):
    rows, d = mem2.shape
    mlen = rows // batch
    dk = X_HEADS * X_DH
    return pl.pallas_call(
        _xkv_kernel,
        grid=(batch,),
        in_specs=[pl.BlockSpec((mlen, d), lambda bb: (bb, 0)),
                  _resident((1, d)), _resident(w.shape), _resident((1, X_DH))],
        out_specs=[pl.BlockSpec((mlen, dk), lambda bb: (bb, 0)), pl.BlockSpec((mlen, dk), lambda bb: (bb, 0))],
        out_shape=[jax.ShapeDtypeStruct((rows, dk), BF16), jax.ShapeDtypeStruct((rows, dk), BF16)],
        compiler_params=_cparams("parallel"),
        name="xattn_kv",
    )(mem2, g.reshape(1, d), w, kg.reshape(1, X_DH))


def _xattn_kernel(h_ref, g_ref, wq_ref, qg_ref, k_ref, v_ref, wo_ref, o_ref):
    h = h_ref[...]
    q = _dot(_rms(h, g_ref[...]).astype(BF16), wq_ref[...])
    scale = 1.0 / math.sqrt(X_DH)
    outs = []
    for hd in range(X_HEADS):
        sl = slice(hd * X_DH, (hd + 1) * X_DH)
        qh = _rms(q[:, sl], qg_ref[...]).astype(BF16)
        s = _dot_nt(qh, k_ref[:, sl]) * scale
        e = jnp.exp(s - jnp.max(s, axis=-1, keepdims=True))
        p = e / jnp.sum(e, axis=-1, keepdims=True)
        outs.append(_dot(p.astype(BF16), v_ref[:, sl]))
    o = jnp.concatenate(outs, axis=-1).astype(BF16)
    o_ref[...] = h + _dot(o, wo_ref[...])


def _xattn(h, g, wq, qg, k, v, wo, batch):
    m, d = h.shape
    tm = ROW_TILE
    per_b = m // batch // tm
    mlen = k.shape[0] // batch
    dk = k.shape[1]
    return pl.pallas_call(
        _xattn_kernel,
        grid=(batch, per_b),
        in_specs=[pl.BlockSpec((tm, d), lambda bb, i: (bb * per_b + i, 0)),
                  _resident((1, d)), _resident(wq.shape), _resident((1, X_DH)),
                  pl.BlockSpec((mlen, dk), lambda bb, i: (bb, 0)),
                  pl.BlockSpec((mlen, dk), lambda bb, i: (bb, 0)),
                  _resident(wo.shape)],
        out_specs=pl.BlockSpec((tm, d), lambda bb, i: (bb * per_b + i, 0)),
        out_shape=jax.ShapeDtypeStruct((m, d), F32),
        compiler_params=_cparams("parallel", "parallel"),
        name="xattn",
    )(h, g.reshape(1, d), wq, qg.reshape(1, X_DH), k, v, wo)


def _inproj_layout(d_conv, d_sgu):
    cols_q = NSA_HEADS * NSA_DH
    kv0 = 2 * d_conv + 2 * d_sgu + cols_q
    gdh = NSA_KV_GROUPS * NSA_DH
    gate0 = kv0 + 6 * gdh
    merge0 = gate0 + NSA_HEADS * 3
    d = np.arange(NSA_DH)
    kvs = np.concatenate([kv0 + i * gdh + g * NSA_DH + d
                          for g in range(NSA_KV_GROUPS) for i in (2, 4, 3, 5)])
    cmp = kv0 + np.arange(2 * gdh)
    return kv0, kvs, cmp, gate0, merge0


def _prep_w_in(w_in, d_conv, d_sgu):
    kv0, kvs, cmp, gate0, merge0 = _inproj_layout(d_conv, d_sgu)
    w_in = w_in.astype(BF16)
    per_g = NSA_HPG * 3
    gate_blocks = []
    for g in range(NSA_KV_GROUPS):
        blk = w_in[:, :, gate0 + g * per_g:gate0 + (g + 1) * per_g]
        gate_blocks.append(jnp.pad(blk, ((0, 0), (0, 0), (0, 128 - per_g))))
    w_all = jnp.concatenate([w_in[:, :, :kv0], w_in[:, :, kvs], w_in[:, :, cmp]] + gate_blocks
                            + [w_in[:, :, merge0:]], axis=-1)
    d_model = w_in.shape[1]
    widths = (2 * d_conv, 2 * d_sgu, NSA_HEADS * NSA_DH, len(kvs), len(cmp), 128 * NSA_KV_GROUPS, 3 * d_model)
    return w_all, widths


def kernel(x, mem, norm_ffn1, ffn1_wi, ffn1_wo, norm_mix, w_in, conv_w, conv_b, conv_ln_g, conv_ln_b, conv_out,
           sgu_norm, sgu_ws, sgu_b, sgu_out, nsa_q_norm, nsa_k_norm, cmp_pe, cmp_w1, cmp_w2, nsa_out, w_out,
           norm_xattn, mem_norm, xq, xkv, xq_norm, xk_norm, xo, norm_ffn2, ffn2_wi, ffn2_wo):
    batch, t, d_model = x.shape
    depth = w_in.shape[0]
    d_conv = conv_w.shape[-1]
    d_sgu = sgu_norm.shape[-1]
    m = batch * t

    w_all, widths = _prep_w_in(w_in, d_conv, d_sgu)
    bf = lambda w: w.astype(BF16)
    ffn1_wi, ffn1_wo, ffn2_wi, ffn2_wo = bf(ffn1_wi), bf(ffn1_wo), bf(ffn2_wi), bf(ffn2_wo)
    conv_out, sgu_out, nsa_out, w_out = bf(conv_out), bf(sgu_out), bf(nsa_out), bf(w_out)
    xq, xkv, xo = bf(xq), bf(xkv), bf(xo)
    sgu_b_exp = jnp.repeat(jnp.swapaxes(sgu_b, 1, 2), d_sgu // SGU_GROUPS, axis=2)
    q_extras = _nsa_q_extras()
    pe_flat = cmp_pe.reshape(depth, 2, CMP_LEN * NSA_DH)

    h = x.reshape(m, d_model)
    mem2 = mem.reshape(-1, d_model)
    for l in range(depth):
        h = _ffn(h, norm_ffn1[l], ffn1_wi[l], ffn1_wo[l])

        z_conv, z_sgu, z_q, z_kvs, z_cmp, z_gate, z_merge = _inproj(h, norm_mix[l], w_all[l], widths)
        act_a = _conv(z_conv, conv_w[l], conv_b[l], conv_ln_g[l], conv_ln_b[l], batch)
        act_b = _sgu(z_sgu, sgu_norm[l], sgu_ws[l], sgu_b_exp[l])

        qg2 = jnp.tile(nsa_q_norm[l], 2).reshape(1, 2 * NSA_DH)
        kg2 = jnp.concatenate([nsa_k_norm[l, 1], nsa_k_norm[l, 2]]).reshape(1, 2 * NSA_DH)
        qt, ks, kw, vst, vwt = _nsa_prep(z_q, z_kvs, qg2, kg2, q_extras, batch)
        x16 = (z_cmp.reshape(batch, t // CMP_STRIDE, CMP_STRIDE, 2, NSA_KV_GROUPS, NSA_DH)
               .transpose(0, 3, 4, 1, 2, 5)
               .reshape(batch, 2, NSA_KV_GROUPS, t // CMP_STRIDE, CMP_STRIDE * NSA_DH))
        kc, vct = _nsa_cmp(x16, pe_flat[l], cmp_w1[l], cmp_w2[l], nsa_k_norm[l, 0])
        act_c = _nsa_attn(qt, ks, kw, vst, vwt, kc, vct, z_gate, batch)

        h = _merge(h, act_a, act_b, act_c, z_merge, conv_out[l], sgu_out[l], nsa_out[l], w_out[l])

        xk, xv = _xkv(mem2, mem_norm[l], xkv[l], xk_norm[l], batch)
        h = _xattn(h, norm_xattn[l], xq[l], xq_norm[l], xk, xv, xo[l], batch)

        h = _ffn(h, norm_ffn2[l], ffn2_wi[l], ffn2_wo[l])
    return h.reshape(batch, t, d_model)
```

```python
import functools
import math

import numpy as np
import jax
import jax.numpy as jnp
from jax import lax
from jax.experimental import pallas as pl
from jax.experimental.pallas import tpu as pltpu

F32 = jnp.float32
BF16 = jnp.bfloat16

EPS = 1e-6
NEG_INF = -1e30
FORCE_SCORE = 1e4
LOG2E = math.log2(math.e)

CONV_WIDTH = 31
CONV_HALO = 32
SUBLANES = 8
SGU_CHUNK = 128
SGU_GROUPS = 4
NSA_HEADS = 8
NSA_KV_GROUPS = 2
NSA_HPG = NSA_HEADS // NSA_KV_GROUPS
NSA_DH = 64
CMP_LEN = 32
CMP_STRIDE = 16
SEL_BLOCK = 64
TOP_N = 16
WINDOW = 512
Q_BLOCK = 128
X_HEADS = 4
X_DH = 128

AUG_W = 128
AUG_A = NSA_DH
AUG_B = NSA_DH + 3
AUG_N = NSA_DH + 6
AUG_FLAG = NSA_DH + 9
MASK_OFF = -(2.0 ** 100)
V_ROWS = NSA_DH + 16

ROW_TILE = 512
INPROJ_ROW_TILE = 256
SEL_KEY_CHUNK = 256
VMEM_LIMIT = 56 * 1024 * 1024


def _cparams(*sem):
    return pltpu.CompilerParams(dimension_semantics=sem, vmem_limit_bytes=VMEM_LIMIT)


def _resident(shape):
    nd = len(shape)
    return pl.BlockSpec(shape, lambda *_: (0,) * nd, pipeline_mode=pl.Buffered(1))


def _rms(x, g):
    return x * lax.rsqrt(jnp.mean(x * x, axis=-1, keepdims=True) + EPS) * g


def _dot(a, b):
    return jnp.dot(a, b, preferred_element_type=F32)


def _dot_nt(a, b):
    return lax.dot_general(a, b, (((1,), (1,)), ((), ())), preferred_element_type=F32)


def _ffn_kernel(x_ref, g_ref, wi_ref, wo_ref, o_ref, *, d_ff, n_chunks):
    x = x_ref[...]
    xn = _rms(x, g_ref[...]).astype(BF16)
    ck = d_ff // n_chunks
    acc = x
    for c in range(n_chunks):
        a = _dot(xn, wi_ref[:, c * ck:(c + 1) * ck])
        b = _dot(xn, wi_ref[:, d_ff + c * ck:d_ff + (c + 1) * ck])
        mid = (a * jax.nn.sigmoid(a) * b).astype(BF16)
        acc = acc + 0.5 * _dot(mid, wo_ref[c * ck:(c + 1) * ck, :])
    o_ref[...] = acc


def _ffn(h, g, wi, wo):
    m, d = h.shape
    d_ff = wo.shape[0]
    tm = ROW_TILE
    return pl.pallas_call(
        functools.partial(_ffn_kernel, d_ff=d_ff, n_chunks=2),
        grid=(m // tm,),
        in_specs=[pl.BlockSpec((tm, d), lambda i: (i, 0)),
                  _resident((1, d)), _resident(wi.shape), _resident(wo.shape)],
        out_specs=pl.BlockSpec((tm, d), lambda i: (i, 0)),
        out_shape=jax.ShapeDtypeStruct((m, d), F32),
        compiler_params=_cparams("parallel"),
        name="ffn",
    )(h, g.reshape(1, d), wi, wo)


def _inproj_kernel(x_ref, g_ref, w_ref, *o_refs):
    xn = _rms(x_ref[...], g_ref[...]).astype(BF16)
    off = 0
    for o_ref in o_refs:
        n = o_ref.shape[1]
        o_ref[...] = _dot(xn, w_ref[:, off:off + n]).astype(o_ref.dtype)
        off += n


def _inproj(h, g, w, widths, dtypes):
    m, d = h.shape
    tm = INPROJ_ROW_TILE
    return pl.pallas_call(
        _inproj_kernel,
        grid=(m // tm,),
        in_specs=[pl.BlockSpec((tm, d), lambda i: (i, 0)), _resident((1, d)), _resident(w.shape)],
        out_specs=[pl.BlockSpec((tm, n), lambda i: (i, 0)) for n in widths],
        out_shape=[jax.ShapeDtypeStruct((m, n), dt) for n, dt in zip(widths, dtypes)],
        compiler_params=_cparams("parallel"),
        name="inproj",
    )(h, g.reshape(1, d), w)


def _conv_kernel(cur_ref, prev_ref, w_ref, b_ref, lg_ref, lb_ref, o_ref, buf_ref, *, d_conv):
    i = pl.program_id(1)
    tt = cur_ref.shape[0]
    cur = cur_ref[...]
    prev = prev_ref[...]
    keep = (i > 0).astype(F32)
    buf_ref[0:CONV_HALO, :] = prev[:, :d_conv] * jax.nn.sigmoid(prev[:, d_conv:]) * keep
    buf_ref[CONV_HALO:CONV_HALO + tt, :] = cur[:, :d_conv] * jax.nn.sigmoid(cur[:, d_conv:])
    buf_ref[CONV_HALO + tt:, :] = jnp.zeros((SUBLANES, d_conv), F32)
    base = CONV_HALO - (CONV_WIDTH - 1)
    acc = None
    for r in range(SUBLANES):
        part = None
        for k in range(CONV_WIDTH):
            if (base + k) % SUBLANES == r:
                start = base + k - r
                term = w_ref[k:k + 1, :] * buf_ref[start:start + tt + SUBLANES, :]
                part = term if part is None else part + term
        part = part[r:r + tt]
        acc = part if acc is None else acc + part
    y = acc + b_ref[...]
    mu = jnp.mean(y, axis=-1, keepdims=True)
    yc = y - mu
    var = jnp.mean(yc * yc, axis=-1, keepdims=True)
    yn = yc * lax.rsqrt(var + EPS) * lg_ref[...] + lb_ref[...]
    o_ref[...] = (yn * jax.nn.sigmoid(yn)).astype(o_ref.dtype)


def _conv(z, w, b, lg, lb, batch):
    m, two_d = z.shape
    d_conv = two_d // 2
    t = m // batch
    tt = ROW_TILE
    per_b = t // tt
    halo_per_tile = tt // CONV_HALO
    return pl.pallas_call(
        functools.partial(_conv_kernel, d_conv=d_conv),
        grid=(batch, per_b),
        in_specs=[pl.BlockSpec((tt, two_d), lambda bb, i: (bb * per_b + i, 0)),
                  pl.BlockSpec((CONV_HALO, two_d),
                               lambda bb, i: (jnp.maximum((bb * per_b + i) * halo_per_tile - 1, 0), 0)),
                  _resident(w.shape), _resident((1, d_conv)), _resident((1, d_conv)), _resident((1, d_conv))],
        out_specs=pl.BlockSpec((tt, d_conv), lambda bb, i: (bb * per_b + i, 0)),
        out_shape=jax.ShapeDtypeStruct((m, d_conv), BF16),
        scratch_shapes=[pltpu.VMEM((CONV_HALO + tt + SUBLANES, d_conv), F32)],
        compiler_params=_cparams("parallel", "parallel"),
        name="conv",
    )(z, z, w, b.reshape(1, -1), lg.reshape(1, -1), lb.reshape(1, -1))


def _sgu_kernel(z_ref, g_ref, ws_ref, b_ref, o_ref, *, d_sgu):
    tm = z_ref.shape[0]
    gd = d_sgu // SGU_GROUPS
    ge = jax.nn.gelu(z_ref[...])
    u = ge[:, :d_sgu]
    vn = _rms(ge[:, d_sgu:], g_ref[...]).astype(BF16)
    row = lax.broadcasted_iota(jnp.int32, (SGU_CHUNK, SGU_CHUNK), 0)
    col = lax.broadcasted_iota(jnp.int32, (SGU_CHUNK, SGU_CHUNK), 1)
    for gr in range(SGU_GROUPS):
        wt = jnp.where(row >= col, ws_ref[gr], 0.0).astype(BF16)
        for ch in range(tm // SGU_CHUNK):
            rows = slice(ch * SGU_CHUNK, (ch + 1) * SGU_CHUNK)
            cols = slice(gr * gd, (gr + 1) * gd)
            mixed = _dot(wt, vn[rows, cols]) + b_ref[:, cols]
            o_ref[rows, cols] = (u[rows, cols] * mixed).astype(o_ref.dtype)


def _sgu(z, g, ws, b_exp):
    m, two_d = z.shape
    d_sgu = two_d // 2
    tm = ROW_TILE
    return pl.pallas_call(
        functools.partial(_sgu_kernel, d_sgu=d_sgu),
        grid=(m // tm,),
        in_specs=[pl.BlockSpec((tm, two_d), lambda i: (i, 0)),
                  _resident((1, d_sgu)), _resident(ws.shape), _resident(b_exp.shape)],
        out_specs=pl.BlockSpec((tm, d_sgu), lambda i: (i, 0)),
        out_shape=jax.ShapeDtypeStruct((m, d_sgu), BF16),
        compiler_params=_cparams("parallel"),
        name="sgu",
    )(z, g.reshape(1, -1), ws, b_exp)


def _half_rms(x, gains):
    lane = lax.broadcasted_iota(jnp.int32, x.shape, 1)
    lo = lane < NSA_DH
    x2 = x * x
    s_lo = jnp.sum(jnp.where(lo, x2, 0.0), axis=-1, keepdims=True)
    s_hi = jnp.sum(jnp.where(lo, 0.0, x2), axis=-1, keepdims=True)
    inv = lax.rsqrt(jnp.where(lo, s_lo, s_hi) * (1.0 / NSA_DH) + EPS)
    return x * inv * gains


def _nsa_prep_kernel(q_ref, kv_ref, qg_ref, kg_ref, ex_ref, qt_ref, ks_ref, kw_ref, vst_ref, vwt_ref):
    g = pl.program_id(1)
    i = pl.program_id(2)
    tp = q_ref.shape[0]
    pair_w = 2 * NSA_DH
    lane = lax.broadcasted_iota(jnp.int32, (tp, AUG_W), 1)
    head_lanes = lane < NSA_DH
    is_pad = i == 0

    q = q_ref[...]
    q_scale = LOG2E / math.sqrt(NSA_DH)
    for hp in range(NSA_HPG // 2):
        pair = _half_rms(q[:, hp * pair_w:(hp + 1) * pair_w], qg_ref[...]) * q_scale
        for sub in range(2):
            h = 2 * hp + sub
            x = pair if sub == 0 else pltpu.roll(pair, NSA_DH, 1)
            qa = jnp.where(head_lanes, x, ex_ref[pl.ds(g * NSA_HPG + h, 1), :])
            qat = qa.T.astype(qt_ref.dtype)
            for cb in range(tp // Q_BLOCK):
                col = (cb * NSA_HPG + h) * Q_BLOCK
                qt_ref[:, col:col + Q_BLOCK] = qat[:, cb * Q_BLOCK:(cb + 1) * Q_BLOCK]

    kv = kv_ref[...]
    kpair = _half_rms(kv[:, :pair_w], kg_ref[...])
    pos = (i - 1) * tp + lax.broadcasted_iota(jnp.int32, (tp, 1), 0)
    blk = lax.shift_right_arithmetic(pos, int(math.log2(SEL_BLOCK))).astype(F32)
    off = (pos & (SEL_BLOCK - 1)).astype(F32)
    aug = jnp.where((lane >= AUG_A) & (lane < AUG_A + 3), blk,
                    jnp.where((lane >= AUG_B) & (lane < AUG_B + 3), off, 0.0))
    pad_key = jnp.where(lane == AUG_FLAG, MASK_OFF, 0.0)
    ks_ref[...] = jnp.where(head_lanes, kpair, aug).astype(ks_ref.dtype)
    kw = jnp.where(head_lanes, pltpu.roll(kpair, NSA_DH, 1), aug)
    kw_ref[...] = jnp.where(is_pad, pad_key, kw).astype(kw_ref.dtype)

    vpair = kv[:, pair_w:]
    ones_lane = jnp.where(lane == NSA_DH, 1.0, 0.0)
    vst_ref[...] = jnp.where(head_lanes, vpair, ones_lane).T[:V_ROWS].astype(vst_ref.dtype)
    vw = jnp.where(head_lanes, pltpu.roll(vpair, NSA_DH, 1), ones_lane)
    vwt_ref[...] = jnp.where(is_pad, 0.0, vw).T[:V_ROWS].astype(vwt_ref.dtype)


def _nsa_prep(zq, zkvs, qg2, kg2, q_extras, batch):
    m = zq.shape[0]
    t = m // batch
    tp = WINDOW
    per_b = t // tp
    gw = NSA_HPG * NSA_DH
    g_n = NSA_KV_GROUPS

    def in_map(bb, g, i):
        return (bb * per_b + jnp.maximum(i - 1, 0), g)

    def rows_now(bb, g, i):
        return (bb, g, jnp.maximum(i - 1, 0), 0)

    def cols_now(bb, g, i):
        return (bb, g, 0, jnp.maximum(i - 1, 0))

    return pl.pallas_call(
        _nsa_prep_kernel,
        grid=(batch, g_n, per_b + 1),
        in_specs=[pl.BlockSpec((tp, gw), in_map), pl.BlockSpec((tp, gw), in_map),
                  _resident((1, 2 * NSA_DH)), _resident((1, 2 * NSA_DH)), _resident(q_extras.shape)],
        out_specs=[pl.BlockSpec((None, None, AUG_W, tp * NSA_HPG), cols_now),
                   pl.BlockSpec((None, None, tp, AUG_W), rows_now),
                   pl.BlockSpec((None, None, tp, AUG_W), lambda bb, g, i: (bb, g, i, 0)),
                   pl.BlockSpec((None, None, V_ROWS, tp), cols_now),
                   pl.BlockSpec((None, None, V_ROWS, tp), lambda bb, g, i: (bb, g, 0, i))],
        out_shape=[jax.ShapeDtypeStruct((batch, g_n, AUG_W, t * NSA_HPG), BF16),
                   jax.ShapeDtypeStruct((batch, g_n, t, AUG_W), BF16),
                   jax.ShapeDtypeStruct((batch, g_n, WINDOW + t, AUG_W), BF16),
                   jax.ShapeDtypeStruct((batch, g_n, V_ROWS, t), BF16),
                   jax.ShapeDtypeStruct((batch, g_n, V_ROWS, WINDOW + t), BF16)],
        compiler_params=_cparams("parallel", "parallel", "arbitrary"),
        name="nsa_prep",
    )(zq, zkvs, qg2, kg2, q_extras)


def _nsa_cmp_kernel(zk_ref, zv_ref, pe_ref, w1_ref, w2_ref, kg_ref, kc_ref, vct_ref):
    rows = kc_ref.shape[1]
    pair_w = zk_ref.shape[1]
    rid = lax.broadcasted_iota(jnp.int32, (rows, 1), 0)

    def mlp(z_ref, kind):
        first = jnp.zeros((rows, pair_w), F32)
        second = jnp.zeros((rows, pair_w), F32)
        for l in range(CMP_STRIDE):
            x = z_ref[pl.ds(l, rows, stride=CMP_STRIDE), :]
            first = first + _dot((x + pe_ref[kind, l:l + 1, :]).astype(BF16), w1_ref[kind, l])
            second = second + _dot((x + pe_ref[kind, CMP_STRIDE + l:CMP_STRIDE + l + 1, :]).astype(BF16),
                                   w1_ref[kind, CMP_STRIDE + l])
        hid = first + jnp.where(rid < rows - 1, pltpu.roll(second, rows - 1, 0), 0.0)
        return _dot((hid * jax.nn.sigmoid(hid)).astype(BF16), w2_ref[kind])

    lane = lax.broadcasted_iota(jnp.int32, (rows, AUG_W), 1)
    head_lanes = lane < NSA_DH
    aug = jnp.where((lane >= AUG_N) & (lane < AUG_N + 3), rid.astype(F32), 0.0)
    kc_pair = _half_rms(mlp(zk_ref, 0), kg_ref[...])
    kc_ref[0] = jnp.where(head_lanes, kc_pair, aug).astype(kc_ref.dtype)
    kc_ref[1] = jnp.where(head_lanes, pltpu.roll(kc_pair, NSA_DH, 1), aug).astype(kc_ref.dtype)
    vct = mlp(zv_ref, 1).T.astype(vct_ref.dtype)
    vct_ref[0] = vct[:NSA_DH]
    vct_ref[1] = vct[NSA_DH:]


def _nsa_cmp(z_cmp, pe_rows, w1_bd, w2_bd, kg2, batch):
    m = z_cmp.shape[0]
    t = m // batch
    rows = t // CMP_STRIDE
    g = NSA_KV_GROUPS
    pair_w = g * NSA_DH
    return pl.pallas_call(
        _nsa_cmp_kernel,
        grid=(batch,),
        in_specs=[pl.BlockSpec((t, pair_w), lambda bb: (bb, 0)), pl.BlockSpec((t, pair_w), lambda bb: (bb, 1)),
                  _resident(pe_rows.shape), _resident(w1_bd.shape), _resident(w2_bd.shape),
                  _resident((1, pair_w))],
        out_specs=[pl.BlockSpec((None, g, rows, AUG_W), lambda bb: (bb, 0, 0, 0)),
                   pl.BlockSpec((None, g, NSA_DH, rows), lambda bb: (bb, 0, 0, 0))],
        out_shape=[jax.ShapeDtypeStruct((batch, g, rows, AUG_W), BF16),
                   jax.ShapeDtypeStruct((batch, g, NSA_DH, rows), BF16)],
        compiler_params=_cparams("parallel"),
        name="nsa_cmp",
    )(z_cmp, z_cmp, pe_rows, w1_bd, w2_bd, kg2)


def _group_block_diag(w):
    eye = jnp.eye(NSA_KV_GROUPS, dtype=w.dtype)
    bd = jnp.einsum('...io,st->...sito', w, eye)
    return bd.reshape(bd.shape[:-4] + (bd.shape[-4] * bd.shape[-3], bd.shape[-2] * bd.shape[-1]))


def _split3_bf16(x):
    hi = x.astype(BF16)
    r1 = x - hi.astype(F32)
    mid = r1.astype(BF16)
    lo = (r1 - mid.astype(F32)).astype(BF16)
    return hi, mid, lo


def _top_n_mask(val, valid):
    n_sel = val.shape[0]
    groups = [val[8 * k:8 * k + 8] for k in range(n_sel // 8)]
    ranks = [jnp.zeros(g.shape, F32) for g in groups]
    sub = lax.broadcasted_iota(jnp.int32, groups[0].shape, 0)
    for i in range(n_sel):
        vi = jnp.broadcast_to(val[i:i + 1, :], groups[0].shape)
        for k, vk in enumerate(groups):
            if 8 * k > i:
                ahead = vi >= vk
            elif 8 * k + 7 < i:
                ahead = vi > vk
            else:
                ahead = (vi > vk) | ((vi == vk) & (sub + 8 * k > i))
            ranks[k] = ranks[k] + jnp.where(ahead, 1.0, 0.0)
    rank = jnp.concatenate(ranks, axis=0)
    return (rank < TOP_N) & valid


def _nsa_attn_kernel(qt_ref, ks_ref, kw_ref, vst_ref, vwt_ref, kc_ref, vct_ref, gate_ref, o_ref,
                     selneg_ref, sa_ref, sb_ref, pa_ref, pb_ref, m_ref, acc_ref):
    c = pl.program_id(2)
    nq = Q_BLOCK
    n_sel = ks_ref.shape[0] // SEL_BLOCK
    n_cmp_rows = kc_ref.shape[0]
    ck = SEL_KEY_CHUNK
    wk = WINDOW + nq
    qt = qt_ref[...]
    q_lane = lax.broadcasted_iota(jnp.int32, (1, nq), 1)
    t_lane = c * nq + q_lane

    def heads(x):
        return [x[:, h * nq:(h + 1) * nq] for h in range(NSA_HPG)]

    def add_per_head(x, bias):
        return jnp.concatenate([xh + bias for xh in heads(x)], axis=1)

    def normalised(acc):
        return acc[:NSA_DH] * (1.0 / acc[NSA_DH:NSA_DH + 1])

    w0 = pl.multiple_of(c * nq, nq)
    e_row = lax.broadcasted_iota(jnp.int32, (nq, 1), 0)
    s_c_raw = _dot(kc_ref[...], qt)
    s_w = _dot(kw_ref[pl.ds(w0, wk), :], qt)
    sa_ref[...] = _dot(ks_ref[0:ck, :], qt)

    def window_output():
        s = jnp.concatenate([add_per_head(s_w[:nq], jnp.where(e_row > q_lane, 0.0, NEG_INF)),
                             s_w[nq:WINDOW],
                             add_per_head(s_w[WINDOW:], jnp.where(e_row <= q_lane, 0.0, NEG_INF))], axis=0)
        p = jnp.exp2(s - jnp.max(s, axis=0, keepdims=True)).astype(BF16)
        return normalised(_dot(vwt_ref[:, pl.ds(w0, wk)], p))

    n_row = lax.broadcasted_iota(jnp.int32, (n_cmp_rows, 1), 0)
    ok_c = CMP_STRIDE * n_row + (CMP_LEN - 1) <= t_lane
    any_c = jnp.where(t_lane >= CMP_LEN - 1, 1.0, 0.0)
    s_c = add_per_head(s_c_raw, jnp.where(ok_c, 0.0, NEG_INF))
    e_c = jnp.exp2(s_c - jnp.max(s_c, axis=0, keepdims=True))
    inv_c = jnp.concatenate([any_c] * NSA_HPG, axis=1) / jnp.sum(e_c, axis=0, keepdims=True)
    p_c = e_c * inv_c
    o_cmp = _dot(vct_ref[...], p_c.astype(BF16))
    p_sum = sum(heads(p_c))

    j_row = lax.broadcasted_iota(jnp.int32, (n_sel, 1), 0)
    n_lane = lax.broadcasted_iota(jnp.int32, (1, n_cmp_rows), 1)
    ratio = SEL_BLOCK // CMP_STRIDE
    first_cmp = ratio * j_row - (CMP_LEN - 1) // CMP_STRIDE
    overlap_t = (n_lane >= first_cmp) & (n_lane < ratio * (j_row + 1)) & (n_lane < n_cmp_rows - 1)
    overlap_t = jnp.where(overlap_t, 1.0, 0.0).astype(BF16)
    imp_t = sum(_dot(overlap_t, part) for part in _split3_bf16(p_sum))
    cur = _sel_block_of(t_lane)
    forced = (j_row == 0) | (j_row == cur) | (j_row == cur - 1)
    valid = SEL_BLOCK * j_row <= t_lane
    val = jnp.where(valid, jnp.where(forced, FORCE_SCORE, imp_t), NEG_INF)
    selneg_ref[...] = jnp.where(_top_n_mask(val, valid), 0.0, MASK_OFF)
    o_win = window_output()

    def scores_into(s_ref, k0):
        s_ref[...] = _dot(ks_ref[pl.ds(pl.multiple_of(k0, ck), ck), :], qt)

    def softmax_into(s_ref, p_ref, k0, causal_bias):
        j0 = k0 // SEL_BLOCK
        bias = jnp.concatenate(
            [jnp.broadcast_to(selneg_ref[pl.ds(j0 + b, 1), :], (SEL_BLOCK, nq)) for b in range(ck // SEL_BLOCK)],
            axis=0)
        if causal_bias is not None:
            bias = bias + causal_bias
        s = add_per_head(s_ref[...], bias)
        m_run = m_ref[...]
        m_new = jnp.maximum(m_run, jnp.max(s, axis=0, keepdims=True))
        p_ref[...] = jnp.exp2(s - m_new).astype(p_ref.dtype)
        m_ref[...] = m_new
        return jnp.exp2(m_run - m_new)

    def weighted_values(p_ref, k0):
        return _dot(vst_ref[:, pl.ds(pl.multiple_of(k0, ck), ck)], p_ref[...])

    def fold(alpha, pending):
        acc_ref[...] = alpha * (acc_ref[...] + pending)

    m_ref[...] = jnp.full(m_ref.shape, NEG_INF, F32)
    acc_ref[...] = jnp.zeros(acc_ref.shape, F32)
    pb_ref[...] = jnp.zeros(pb_ref.shape, pb_ref.dtype)
    n_past = (c * nq) // ck

    def two_chunks(i2, carry):
        k0 = pl.multiple_of(2 * i2 * ck, 2 * ck)
        pending = weighted_values(pb_ref, jnp.maximum(k0 - ck, 0))
        scores_into(sb_ref, k0 + ck)
        fold(softmax_into(sa_ref, pa_ref, k0, None), pending)
        pending = weighted_values(pa_ref, k0)
        scores_into(sa_ref, k0 + 2 * ck)
        fold(softmax_into(sb_ref, pb_ref, k0 + ck, None), pending)
        return carry

    lax.fori_loop(0, n_past // 2, two_chunks, 0)
    k_last = pl.multiple_of(n_past * ck, ck)
    key_row = lax.broadcasted_iota(jnp.int32, (ck, 1), 0)
    causal = jnp.where(key_row + k_last <= t_lane, 0.0, NEG_INF)

    @pl.when(n_past % 2 == 1)
    def _():
        pending = weighted_values(pb_ref, jnp.maximum(k_last - 2 * ck, 0))
        scores_into(sb_ref, k_last)
        fold(softmax_into(sa_ref, pa_ref, k_last - ck, None), pending)
        pending = weighted_values(pa_ref, k_last - ck)
        fold(softmax_into(sb_ref, pb_ref, k_last, causal), pending)
        acc_ref[...] = acc_ref[...] + weighted_values(pb_ref, k_last)

    @pl.when(n_past % 2 == 0)
    def _():
        pending = weighted_values(pb_ref, jnp.maximum(k_last - ck, 0))
        fold(softmax_into(sa_ref, pa_ref, k_last, causal), pending)
        acc_ref[...] = acc_ref[...] + weighted_values(pa_ref, k_last)

    o_sel = normalised(acc_ref[...])

    gates_t = jax.nn.sigmoid(gate_ref[...]).T
    outs = [gates_t[3 * h:3 * h + 1] * oc + gates_t[3 * h + 1:3 * h + 2] * osl + gates_t[3 * h + 2:3 * h + 3] * ow
            for h, (oc, osl, ow) in enumerate(zip(heads(o_cmp), heads(o_sel), heads(o_win)))]
    for hp in range(NSA_HPG // 2):
        pair = jnp.concatenate(outs[2 * hp:2 * hp + 2], axis=0)
        o_ref[:, hp * 2 * NSA_DH:(hp + 1) * 2 * NSA_DH] = pair.T.astype(o_ref.dtype)


def _sel_block_of(pos):
    return lax.shift_right_logical(pos, int(math.log2(SEL_BLOCK)))


def _nsa_attn(qt, ks, kw, vst, vwt, kc, vct, zgate, batch):
    t = ks.shape[2]
    m = batch * t
    n_qb = t // Q_BLOCK
    gw = NSA_HPG * NSA_DH
    n_cmp_rows = kc.shape[2]

    def row_map(bb, g, c):
        return (bb * n_qb + c, g)

    def kv_map(bb, g, c):
        return (bb, g, 0, 0)

    def whole(shape):
        return pl.BlockSpec((None, None) + shape, kv_map)

    return pl.pallas_call(
        _nsa_attn_kernel,
        grid=(batch, NSA_KV_GROUPS, n_qb),
        in_specs=[pl.BlockSpec((None, None, AUG_W, NSA_HPG * Q_BLOCK), lambda bb, g, c: (bb, g, 0, c)),
                  whole((t, AUG_W)), whole((WINDOW + t, AUG_W)),
                  whole((V_ROWS, t)), whole((V_ROWS, WINDOW + t)),
                  whole((n_cmp_rows, AUG_W)), whole((NSA_DH, n_cmp_rows)),
                  pl.BlockSpec((Q_BLOCK, 128), row_map)],
        out_specs=pl.BlockSpec((Q_BLOCK, gw), row_map),
        out_shape=jax.ShapeDtypeStruct((m, NSA_KV_GROUPS * gw), BF16),
        scratch_shapes=[pltpu.VMEM((t // SEL_BLOCK, Q_BLOCK), F32),
                        pltpu.VMEM((SEL_KEY_CHUNK, NSA_HPG * Q_BLOCK), F32),
                        pltpu.VMEM((SEL_KEY_CHUNK, NSA_HPG * Q_BLOCK), F32),
                        pltpu.VMEM((SEL_KEY_CHUNK, NSA_HPG * Q_BLOCK), BF16),
                        pltpu.VMEM((SEL_KEY_CHUNK, NSA_HPG * Q_BLOCK), BF16),
                        pltpu.VMEM((1, NSA_HPG * Q_BLOCK), F32),
                        pltpu.VMEM((V_ROWS, NSA_HPG * Q_BLOCK), F32)],
        compiler_params=_cparams("parallel", "parallel", "arbitrary"),
        name="nsa_attn",
    )(qt, ks, kw, vst, vwt, kc, vct, zgate)


def _nsa_q_extras():
    slopes = jnp.asarray(2.0 ** (-8.0 * np.arange(1, NSA_HEADS + 1) / NSA_HEADS) * LOG2E, dtype=F32)
    parts = jnp.stack([p.astype(F32) for p in _split3_bf16(slopes)], axis=1)
    ex = jnp.zeros((NSA_HEADS, AUG_W), F32)
    ex = ex.at[:, AUG_A:AUG_A + 3].set(parts * SEL_BLOCK)
    ex = ex.at[:, AUG_B:AUG_B + 3].set(parts)
    ex = ex.at[:, AUG_N:AUG_N + 3].set(parts * CMP_STRIDE)
    return ex.at[:, AUG_FLAG].set(1.0)


def _merge_kernel(h_ref, a_ref, b_ref, c_ref, zm_ref, wa_ref, wb_ref, wc_ref, wo_ref, o_ref):
    d = h_ref.shape[1]
    gates = jax.nn.sigmoid(zm_ref[...].astype(F32))
    y = (gates[:, :d] * _dot(a_ref[...], wa_ref[...])
         + gates[:, d:2 * d] * _dot(b_ref[...], wb_ref[...])
         + gates[:, 2 * d:] * _dot(c_ref[...], wc_ref[...]))
    o_ref[...] = h_ref[...] + _dot(y.astype(BF16), wo_ref[...])


def _merge(h, a, b, c, zm, wa, wb, wc, wo):
    m, d = h.shape
    tm = ROW_TILE

    def rows(n):
        return pl.BlockSpec((tm, n), lambda i: (i, 0))

    return pl.pallas_call(
        _merge_kernel,
        grid=(m // tm,),
        in_specs=[rows(d), rows(a.shape[1]), rows(b.shape[1]), rows(c.shape[1]), rows(zm.shape[1]),
                  _resident(wa.shape), _resident(wb.shape), _resident(wc.shape), _resident(wo.shape)],
        out_specs=rows(d),
        out_shape=jax.ShapeDtypeStruct((m, d), F32),
        compiler_params=_cparams("parallel"),
        name="merge",
    )(h, a, b, c, zm, wa, wb, wc, wo)


def _xkv_kernel(mem_ref, g_ref, w_ref, kg_ref, k_ref, v_ref):
    mn = _rms(mem_ref[...], g_ref[...]).astype(BF16)
    kv = _dot(mn, w_ref[...])
    dk = k_ref.shape[1]
    for h in range(X_HEADS):
        sl = slice(h * X_DH, (h + 1) * X_DH)
        k_ref[:, sl] = _rms(kv[:, sl], kg_ref[...]).astype(k_ref.dtype)
    v_ref[...] = kv[:, dk:].astype(v_ref.dtype)


def _xkv(mem2, g, w, kg, batch):
    rows, d = mem2.shape
    mlen = rows // batch
    dk = X_HEADS * X_DH
    return pl.pallas_call(
        _xkv_kernel,
        grid=(batch,),
        in_specs=[pl.BlockSpec((mlen, d), lambda bb: (bb, 0)),
                  _resident((1, d)), _resident(w.shape), _resident((1, X_DH))],
        out_specs=[pl.BlockSpec((mlen, dk), lambda bb: (bb, 0)), pl.BlockSpec((mlen, dk), lambda bb: (bb, 0))],
        out_shape=[jax.ShapeDtypeStruct((rows, dk), BF16), jax.ShapeDtypeStruct((rows, dk), BF16)],
        compiler_params=_cparams("parallel"),
        name="xattn_kv",
    )(mem2, g.reshape(1, d), w, kg.reshape(1, X_DH))


def _xattn_kernel(h_ref, g_ref, wq_ref, qg_ref, k_ref, v_ref, wo_ref, o_ref):
    h = h_ref[...]
    q = _dot(_rms(h, g_ref[...]).astype(BF16), wq_ref[...])
    scale = 1.0 / math.sqrt(X_DH)
    outs = []
    for hd in range(X_HEADS):
        sl = slice(hd * X_DH, (hd + 1) * X_DH)
        qh = _rms(q[:, sl], qg_ref[...]).astype(BF16)
        s = _dot_nt(qh, k_ref[:, sl]) * scale
        e = jnp.exp(s - jnp.max(s, axis=-1, keepdims=True))
        p = e / jnp.sum(e, axis=-1, keepdims=True)
        outs.append(_dot(p.astype(BF16), v_ref[:, sl]))
    o = jnp.concatenate(outs, axis=-1).astype(BF16)
    o_ref[...] = h + _dot(o, wo_ref[...])


def _xattn(h, g, wq, qg, k, v, wo, batch):
    m, d = h.shape
    tm = ROW_TILE
    per_b = m // batch // tm
    mlen = k.shape[0] // batch
    dk = k.shape[1]
    return pl.pallas_call(
        _xattn_kernel,
        grid=(batch, per_b),
        in_specs=[pl.BlockSpec((tm, d), lambda bb, i: (bb * per_b + i, 0)),
                  _resident((1, d)), _resident(wq.shape), _resident((1, X_DH)),
                  pl.BlockSpec((mlen, dk), lambda bb, i: (bb, 0)),
                  pl.BlockSpec((mlen, dk), lambda bb, i: (bb, 0)),
                  _resident(wo.shape)],
        out_specs=pl.BlockSpec((tm, d), lambda bb, i: (bb * per_b + i, 0)),
        out_shape=jax.ShapeDtypeStruct((m, d), F32),
        compiler_params=_cparams("parallel", "parallel"),
        name="xattn",
    )(h, g.reshape(1, d), wq, qg.reshape(1, X_DH), k, v, wo)


def _inproj_layout(d_conv, d_sgu):
    cols_q = NSA_HEADS * NSA_DH
    kv0 = 2 * d_conv + 2 * d_sgu + cols_q
    gdh = NSA_KV_GROUPS * NSA_DH
    gate0 = kv0 + 6 * gdh
    merge0 = gate0 + NSA_HEADS * 3
    d = np.arange(NSA_DH)
    kvs = np.concatenate([kv0 + i * gdh + g * NSA_DH + d
                          for g in range(NSA_KV_GROUPS) for i in (2, 4, 3, 5)])
    cmp = kv0 + np.arange(2 * gdh)
    return kv0, kvs, cmp, gate0, merge0


def _prep_w_in(w_in, d_conv, d_sgu):
    kv0, kvs, cmp, gate0, merge0 = _inproj_layout(d_conv, d_sgu)
    w_in = w_in.astype(BF16)
    per_g = NSA_HPG * 3
    gate_blocks = []
    for g in range(NSA_KV_GROUPS):
        blk = w_in[:, :, gate0 + g * per_g:gate0 + (g + 1) * per_g]
        gate_blocks.append(jnp.pad(blk, ((0, 0), (0, 0), (0, 128 - per_g))))
    w_all = jnp.concatenate([w_in[:, :, :kv0], w_in[:, :, kvs], w_in[:, :, cmp]] + gate_blocks
                            + [w_in[:, :, merge0:]], axis=-1)
    d_model = w_in.shape[1]
    widths = (2 * d_conv, 2 * d_sgu, NSA_HEADS * NSA_DH, len(kvs), len(cmp), 128 * NSA_KV_GROUPS, 3 * d_model)
    dtypes = (F32,) * (len(widths) - 1) + (BF16,)
    return w_all, widths, dtypes


def kernel(x, mem, norm_ffn1, ffn1_wi, ffn1_wo, norm_mix, w_in, conv_w, conv_b, conv_ln_g, conv_ln_b, conv_out,
           sgu_norm, sgu_ws, sgu_b, sgu_out, nsa_q_norm, nsa_k_norm, cmp_pe, cmp_w1, cmp_w2, nsa_out, w_out,
           norm_xattn, mem_norm, xq, xkv, xq_norm, xk_norm, xo, norm_ffn2, ffn2_wi, ffn2_wo):
    batch, t, d_model = x.shape
    depth = w_in.shape[0]
    d_conv = conv_w.shape[-1]
    d_sgu = sgu_norm.shape[-1]
    m = batch * t

    w_all, widths, dtypes = _prep_w_in(w_in, d_conv, d_sgu)
    bf = lambda w: w.astype(BF16)
    ffn1_wi, ffn1_wo, ffn2_wi, ffn2_wo = bf(ffn1_wi), bf(ffn1_wo), bf(ffn2_wi), bf(ffn2_wo)
    conv_out, sgu_out, nsa_out, w_out = bf(conv_out), bf(sgu_out), bf(nsa_out), bf(w_out)
    xq, xkv, xo = bf(xq), bf(xkv), bf(xo)
    sgu_b_exp = jnp.repeat(jnp.swapaxes(sgu_b, 1, 2), d_sgu // SGU_GROUPS, axis=2)
    q_extras = _nsa_q_extras()
    pe_rows = jnp.tile(cmp_pe, (1, 1, 1, NSA_KV_GROUPS))
    w1_bd = _group_block_diag(cmp_w1.reshape(depth, 2, CMP_LEN, NSA_DH, NSA_DH)).astype(BF16)
    w2_bd = _group_block_diag(cmp_w2).astype(BF16)
    kg0_2 = jnp.tile(nsa_k_norm[:, 0], (1, 2)).reshape(depth, 1, 2 * NSA_DH)

    h = x.reshape(m, d_model)
    mem2 = mem.reshape(-1, d_model)
    for l in range(depth):
        h = _ffn(h, norm_ffn1[l], ffn1_wi[l], ffn1_wo[l])

        z_conv, z_sgu, z_q, z_kvs, z_cmp, z_gate, z_merge = _inproj(h, norm_mix[l], w_all[l], widths, dtypes)
        act_a = _conv(z_conv, conv_w[l], conv_b[l], conv_ln_g[l], conv_ln_b[l], batch)
        act_b = _sgu(z_sgu, sgu_norm[l], sgu_ws[l], sgu_b_exp[l])

        qg2 = jnp.tile(nsa_q_norm[l], 2).reshape(1, 2 * NSA_DH)
        kg2 = jnp.concatenate([nsa_k_norm[l, 1], nsa_k_norm[l, 2]]).reshape(1, 2 * NSA_DH)
        qt, ks, kw, vst, vwt = _nsa_prep(z_q, z_kvs, qg2, kg2, q_extras, batch)
        kc, vct = _nsa_cmp(z_cmp, pe_rows[l], w1_bd[l], w2_bd[l], kg0_2[l], batch)
        act_c = _nsa_attn(qt, ks, kw, vst, vwt, kc, vct, z_gate, batch)

        h = _merge(h, act_a, act_b, act_c, z_merge, conv_out[l], sgu_out[l], nsa_out[l], w_out[l])

        xk, xv = _xkv(mem2, mem_norm[l], xkv[l], xk_norm[l], batch)
        h = _xattn(h, norm_xattn[l], xq[l], xq_norm[l], xk, xv, xo[l], batch)

        h = _ffn(h, norm_ffn2[l], ffn2_wi[l], ffn2_wo[l])
    return h.reshape(batch, t, d_model)
```

```python
import functools
import math

import numpy as np
import jax
import jax.numpy as jnp
from jax import lax
from jax.experimental import pallas as pl
from jax.experimental.pallas import tpu as pltpu

F32 = jnp.float32
BF16 = jnp.bfloat16

EPS = 1e-6
NEG_INF = -1e30
FORCE_SCORE = 1e4
LOG2E = math.log2(math.e)

CONV_WIDTH = 31
CONV_HALO = 32
SUBLANES = 8
SGU_CHUNK = 128
SGU_GROUPS = 4
NSA_HEADS = 8
NSA_KV_GROUPS = 2
NSA_HPG = NSA_HEADS // NSA_KV_GROUPS
NSA_DH = 64
CMP_LEN = 32
CMP_STRIDE = 16
SEL_BLOCK = 64
TOP_N = 16
WINDOW = 512
Q_BLOCK = 128
X_HEADS = 4
X_DH = 128

AUG_W = 128
AUG_A = NSA_DH
AUG_B = NSA_DH + 3
AUG_N = NSA_DH + 6
AUG_FLAG = NSA_DH + 9
MASK_OFF = -(2.0 ** 100)
V_ROWS = NSA_DH + 16

ROW_TILE = 512
INPROJ_ROW_TILE = 256
SEL_KEY_CHUNK = 256
VMEM_LIMIT = 56 * 1024 * 1024


def _cparams(*sem):
    return pltpu.CompilerParams(dimension_semantics=sem, vmem_limit_bytes=VMEM_LIMIT)


def _resident(shape):
    nd = len(shape)
    return pl.BlockSpec(shape, lambda *_: (0,) * nd, pipeline_mode=pl.Buffered(1))


def _layer(stacked_shape, l):
    nd = len(stacked_shape)
    return pl.BlockSpec((None,) + tuple(stacked_shape[1:]), lambda *_: (l,) + (0,) * (nd - 1),
                        pipeline_mode=pl.Buffered(1))


def _rms(x, g):
    return x * lax.rsqrt(jnp.mean(x * x, axis=-1, keepdims=True) + EPS) * g


def _dot(a, b):
    return jnp.dot(a, b, preferred_element_type=F32)


def _dot_nt(a, b):
    return lax.dot_general(a, b, (((1,), (1,)), ((), ())), preferred_element_type=F32)


def _ffn_kernel(x_ref, g_ref, wi_ref, wo_ref, o_ref, *, d_ff, n_chunks):
    x = x_ref[...]
    xn = _rms(x, g_ref[...]).astype(BF16)
    ck = d_ff // n_chunks
    acc = x
    for c in range(n_chunks):
        a = _dot(xn, wi_ref[:, c * ck:(c + 1) * ck])
        b = _dot(xn, wi_ref[:, d_ff + c * ck:d_ff + (c + 1) * ck])
        mid = (a * jax.nn.sigmoid(a) * b).astype(BF16)
        acc = acc + 0.5 * _dot(mid, wo_ref[c * ck:(c + 1) * ck, :])
    o_ref[...] = acc


def _ffn(h, g, wi, wo, l):
    m, d = h.shape
    d_ff = wo.shape[1]
    tm = ROW_TILE
    return pl.pallas_call(
        functools.partial(_ffn_kernel, d_ff=d_ff, n_chunks=2),
        grid=(m // tm,),
        in_specs=[pl.BlockSpec((tm, d), lambda i: (i, 0)),
                  _resident((1, d)), _layer(wi.shape, l), _layer(wo.shape, l)],
        out_specs=pl.BlockSpec((tm, d), lambda i: (i, 0)),
        out_shape=jax.ShapeDtypeStruct((m, d), F32),
        compiler_params=_cparams("parallel"),
        name="ffn",
    )(h, g.reshape(1, d), wi, wo)


def _inproj_kernel(x_ref, g_ref, w_ref, *o_refs):
    xn = _rms(x_ref[...], g_ref[...]).astype(BF16)
    off = 0
    for o_ref in o_refs:
        n = o_ref.shape[1]
        o_ref[...] = _dot(xn, w_ref[:, off:off + n]).astype(o_ref.dtype)
        off += n


def _inproj(h, g, w, l, widths, dtypes):
    m, d = h.shape
    tm = INPROJ_ROW_TILE
    return pl.pallas_call(
        _inproj_kernel,
        grid=(m // tm,),
        in_specs=[pl.BlockSpec((tm, d), lambda i: (i, 0)), _resident((1, d)), _layer(w.shape, l)],
        out_specs=[pl.BlockSpec((tm, n), lambda i: (i, 0)) for n in widths],
        out_shape=[jax.ShapeDtypeStruct((m, n), dt) for n, dt in zip(widths, dtypes)],
        compiler_params=_cparams("parallel"),
        name="inproj",
    )(h, g.reshape(1, d), w)


def _conv_kernel(cur_ref, prev_ref, w_ref, b_ref, lg_ref, lb_ref, o_ref, buf_ref, *, d_conv):
    i = pl.program_id(1)
    tt = cur_ref.shape[0]
    cur = cur_ref[...]
    prev = prev_ref[...]
    keep = (i > 0).astype(F32)
    buf_ref[0:CONV_HALO, :] = prev[:, :d_conv] * jax.nn.sigmoid(prev[:, d_conv:]) * keep
    buf_ref[CONV_HALO:CONV_HALO + tt, :] = cur[:, :d_conv] * jax.nn.sigmoid(cur[:, d_conv:])
    buf_ref[CONV_HALO + tt:, :] = jnp.zeros((SUBLANES, d_conv), F32)
    base = CONV_HALO - (CONV_WIDTH - 1)
    acc = None
    for r in range(SUBLANES):
        part = None
        for k in range(CONV_WIDTH):
            if (base + k) % SUBLANES == r:
                start = base + k - r
                term = w_ref[k:k + 1, :] * buf_ref[start:start + tt + SUBLANES, :]
                part = term if part is None else part + term
        part = part[r:r + tt]
        acc = part if acc is None else acc + part
    y = acc + b_ref[...]
    mu = jnp.mean(y, axis=-1, keepdims=True)
    yc = y - mu
    var = jnp.mean(yc * yc, axis=-1, keepdims=True)
    yn = yc * lax.rsqrt(var + EPS) * lg_ref[...] + lb_ref[...]
    o_ref[...] = (yn * jax.nn.sigmoid(yn)).astype(o_ref.dtype)


def _conv(z, w, b, lg, lb, batch):
    m, two_d = z.shape
    d_conv = two_d // 2
    t = m // batch
    tt = ROW_TILE
    per_b = t // tt
    halo_per_tile = tt // CONV_HALO
    return pl.pallas_call(
        functools.partial(_conv_kernel, d_conv=d_conv),
        grid=(batch, per_b),
        in_specs=[pl.BlockSpec((tt, two_d), lambda bb, i: (bb * per_b + i, 0)),
                  pl.BlockSpec((CONV_HALO, two_d),
                               lambda bb, i: (jnp.maximum((bb * per_b + i) * halo_per_tile - 1, 0), 0)),
                  _resident(w.shape), _resident((1, d_conv)), _resident((1, d_conv)), _resident((1, d_conv))],
        out_specs=pl.BlockSpec((tt, d_conv), lambda bb, i: (bb * per_b + i, 0)),
        out_shape=jax.ShapeDtypeStruct((m, d_conv), BF16),
        scratch_shapes=[pltpu.VMEM((CONV_HALO + tt + SUBLANES, d_conv), F32)],
        compiler_params=_cparams("parallel", "parallel"),
        name="conv",
    )(z, z, w, b.reshape(1, -1), lg.reshape(1, -1), lb.reshape(1, -1))


def _sgu_kernel(z_ref, g_ref, ws_ref, b_ref, o_ref, *, d_sgu):
    tm = z_ref.shape[0]
    gd = d_sgu // SGU_GROUPS
    ge = jax.nn.gelu(z_ref[...])
    u = ge[:, :d_sgu]
    vn = _rms(ge[:, d_sgu:], g_ref[...]).astype(BF16)
    row = lax.broadcasted_iota(jnp.int32, (SGU_CHUNK, SGU_CHUNK), 0)
    col = lax.broadcasted_iota(jnp.int32, (SGU_CHUNK, SGU_CHUNK), 1)
    for gr in range(SGU_GROUPS):
        wt = jnp.where(row >= col, ws_ref[gr], 0.0).astype(BF16)
        for ch in range(tm // SGU_CHUNK):
            rows = slice(ch * SGU_CHUNK, (ch + 1) * SGU_CHUNK)
            cols = slice(gr * gd, (gr + 1) * gd)
            mixed = _dot(wt, vn[rows, cols]) + b_ref[:, cols]
            o_ref[rows, cols] = (u[rows, cols] * mixed).astype(o_ref.dtype)


def _sgu(z, g, ws, b_exp):
    m, two_d = z.shape
    d_sgu = two_d // 2
    tm = ROW_TILE
    return pl.pallas_call(
        functools.partial(_sgu_kernel, d_sgu=d_sgu),
        grid=(m // tm,),
        in_specs=[pl.BlockSpec((tm, two_d), lambda i: (i, 0)),
                  _resident((1, d_sgu)), _resident(ws.shape), _resident(b_exp.shape)],
        out_specs=pl.BlockSpec((tm, d_sgu), lambda i: (i, 0)),
        out_shape=jax.ShapeDtypeStruct((m, d_sgu), BF16),
        compiler_params=_cparams("parallel"),
        name="sgu",
    )(z, g.reshape(1, -1), ws, b_exp)


def _half_rms(x, gains):
    lane = lax.broadcasted_iota(jnp.int32, x.shape, 1)
    lo = lane < NSA_DH
    x2 = x * x
    s_lo = jnp.sum(jnp.where(lo, x2, 0.0), axis=-1, keepdims=True)
    s_hi = jnp.sum(jnp.where(lo, 0.0, x2), axis=-1, keepdims=True)
    inv = lax.rsqrt(jnp.where(lo, s_lo, s_hi) * (1.0 / NSA_DH) + EPS)
    return x * inv * gains


def _nsa_prep_kernel(q_ref, kv_ref, qg_ref, kg_ref, ex_ref, qt_ref, ks_ref, kw_ref, vst_ref, vwt_ref):
    g = pl.program_id(1)
    i = pl.program_id(2)
    tp = q_ref.shape[0]
    pair_w = 2 * NSA_DH
    lane = lax.broadcasted_iota(jnp.int32, (tp, AUG_W), 1)
    head_lanes = lane < NSA_DH
    is_pad = i == 0

    q = q_ref[...]
    q_scale = LOG2E / math.sqrt(NSA_DH)
    for hp in range(NSA_HPG // 2):
        pair = _half_rms(q[:, hp * pair_w:(hp + 1) * pair_w], qg_ref[...]) * q_scale
        for sub in range(2):
            h = 2 * hp + sub
            x = pair if sub == 0 else pltpu.roll(pair, NSA_DH, 1)
            qa = jnp.where(head_lanes, x, ex_ref[pl.ds(g * NSA_HPG + h, 1), :])
            qat = qa.T.astype(qt_ref.dtype)
            for cb in range(tp // Q_BLOCK):
                col = (cb * NSA_HPG + h) * Q_BLOCK
                qt_ref[:, col:col + Q_BLOCK] = qat[:, cb * Q_BLOCK:(cb + 1) * Q_BLOCK]

    kv = kv_ref[...]
    kpair = _half_rms(kv[:, :pair_w], kg_ref[...])
    pos = (i - 1) * tp + lax.broadcasted_iota(jnp.int32, (tp, 1), 0)
    blk = lax.shift_right_arithmetic(pos, int(math.log2(SEL_BLOCK))).astype(F32)
    off = (pos & (SEL_BLOCK - 1)).astype(F32)
    aug = jnp.where((lane >= AUG_A) & (lane < AUG_A + 3), blk,
                    jnp.where((lane >= AUG_B) & (lane < AUG_B + 3), off, 0.0))
    pad_key = jnp.where(lane == AUG_FLAG, MASK_OFF, 0.0)
    ks_ref[...] = jnp.where(head_lanes, kpair, aug).astype(ks_ref.dtype)
    kw = jnp.where(head_lanes, pltpu.roll(kpair, NSA_DH, 1), aug)
    kw_ref[...] = jnp.where(is_pad, pad_key, kw).astype(kw_ref.dtype)

    vpair = kv[:, pair_w:]
    ones_lane = jnp.where(lane == NSA_DH, 1.0, 0.0)
    vst_ref[...] = jnp.where(head_lanes, vpair, ones_lane).T[:V_ROWS].astype(vst_ref.dtype)
    vw = jnp.where(head_lanes, pltpu.roll(vpair, NSA_DH, 1), ones_lane)
    vwt_ref[...] = jnp.where(is_pad, 0.0, vw).T[:V_ROWS].astype(vwt_ref.dtype)


def _nsa_prep(zq, zkvs, qg2, kg2, q_extras, batch):
    m = zq.shape[0]
    t = m // batch
    tp = WINDOW
    per_b = t // tp
    gw = NSA_HPG * NSA_DH
    g_n = NSA_KV_GROUPS

    def in_map(bb, g, i):
        return (bb * per_b + jnp.maximum(i - 1, 0), g)

    def rows_now(bb, g, i):
        return (bb, g, jnp.maximum(i - 1, 0), 0)

    def cols_now(bb, g, i):
        return (bb, g, 0, jnp.maximum(i - 1, 0))

    return pl.pallas_call(
        _nsa_prep_kernel,
        grid=(batch, g_n, per_b + 1),
        in_specs=[pl.BlockSpec((tp, gw), in_map), pl.BlockSpec((tp, gw), in_map),
                  _resident((1, 2 * NSA_DH)), _resident((1, 2 * NSA_DH)), _resident(q_extras.shape)],
        out_specs=[pl.BlockSpec((None, None, AUG_W, tp * NSA_HPG), cols_now),
                   pl.BlockSpec((None, None, tp, AUG_W), rows_now),
                   pl.BlockSpec((None, None, tp, AUG_W), lambda bb, g, i: (bb, g, i, 0)),
                   pl.BlockSpec((None, None, V_ROWS, tp), cols_now),
                   pl.BlockSpec((None, None, V_ROWS, tp), lambda bb, g, i: (bb, g, 0, i))],
        out_shape=[jax.ShapeDtypeStruct((batch, g_n, AUG_W, t * NSA_HPG), BF16),
                   jax.ShapeDtypeStruct((batch, g_n, t, AUG_W), BF16),
                   jax.ShapeDtypeStruct((batch, g_n, WINDOW + t, AUG_W), BF16),
                   jax.ShapeDtypeStruct((batch, g_n, V_ROWS, t), BF16),
                   jax.ShapeDtypeStruct((batch, g_n, V_ROWS, WINDOW + t), BF16)],
        compiler_params=_cparams("parallel", "parallel", "arbitrary"),
        name="nsa_prep",
    )(zq, zkvs, qg2, kg2, q_extras)


def _nsa_cmp_kernel(zk_ref, zv_ref, pe_ref, w1_ref, w2_ref, kg_ref, kc_ref, vct_ref):
    rows = kc_ref.shape[1]
    pair_w = zk_ref.shape[1]
    rid = lax.broadcasted_iota(jnp.int32, (rows, 1), 0)

    def mlp(z_ref, kind):
        first = jnp.zeros((rows, pair_w), F32)
        second = jnp.zeros((rows, pair_w), F32)
        for l in range(CMP_STRIDE):
            x = z_ref[pl.ds(l, rows, stride=CMP_STRIDE), :]
            first = first + _dot((x + pe_ref[kind, l:l + 1, :]).astype(BF16), w1_ref[kind, l])
            second = second + _dot((x + pe_ref[kind, CMP_STRIDE + l:CMP_STRIDE + l + 1, :]).astype(BF16),
                                   w1_ref[kind, CMP_STRIDE + l])
        hid = first + jnp.where(rid < rows - 1, pltpu.roll(second, rows - 1, 0), 0.0)
        return _dot((hid * jax.nn.sigmoid(hid)).astype(BF16), w2_ref[kind])

    lane = lax.broadcasted_iota(jnp.int32, (rows, AUG_W), 1)
    head_lanes = lane < NSA_DH
    aug = jnp.where((lane >= AUG_N) & (lane < AUG_N + 3), rid.astype(F32), 0.0)
    kc_pair = _half_rms(mlp(zk_ref, 0), kg_ref[...])
    kc_ref[0] = jnp.where(head_lanes, kc_pair, aug).astype(kc_ref.dtype)
    kc_ref[1] = jnp.where(head_lanes, pltpu.roll(kc_pair, NSA_DH, 1), aug).astype(kc_ref.dtype)
    vct = mlp(zv_ref, 1).T.astype(vct_ref.dtype)
    vct_ref[0] = vct[:NSA_DH]
    vct_ref[1] = vct[NSA_DH:]


def _nsa_cmp(z_cmp, pe_rows, w1_bd, w2_bd, kg2, l, batch):
    m = z_cmp.shape[0]
    t = m // batch
    rows = t // CMP_STRIDE
    g = NSA_KV_GROUPS
    pair_w = g * NSA_DH
    return pl.pallas_call(
        _nsa_cmp_kernel,
        grid=(batch,),
        in_specs=[pl.BlockSpec((t, pair_w), lambda bb: (bb, 0)), pl.BlockSpec((t, pair_w), lambda bb: (bb, 1)),
                  _resident(pe_rows.shape), _layer(w1_bd.shape, l), _layer(w2_bd.shape, l),
                  _resident((1, pair_w))],
        out_specs=[pl.BlockSpec((None, g, rows, AUG_W), lambda bb: (bb, 0, 0, 0)),
                   pl.BlockSpec((None, g, NSA_DH, rows), lambda bb: (bb, 0, 0, 0))],
        out_shape=[jax.ShapeDtypeStruct((batch, g, rows, AUG_W), BF16),
                   jax.ShapeDtypeStruct((batch, g, NSA_DH, rows), BF16)],
        compiler_params=_cparams("parallel"),
        name="nsa_cmp",
    )(z_cmp, z_cmp, pe_rows, w1_bd, w2_bd, kg2)


def _group_block_diag(w):
    eye = jnp.eye(NSA_KV_GROUPS, dtype=w.dtype)
    bd = jnp.einsum('...io,st->...sito', w, eye)
    return bd.reshape(bd.shape[:-4] + (bd.shape[-4] * bd.shape[-3], bd.shape[-2] * bd.shape[-1]))


def _split3_bf16(x):
    hi = x.astype(BF16)
    r1 = x - hi.astype(F32)
    mid = r1.astype(BF16)
    lo = (r1 - mid.astype(F32)).astype(BF16)
    return hi, mid, lo


def _top_n_mask(val, valid):
    n_sel = val.shape[0]
    groups = [val[8 * k:8 * k + 8] for k in range(n_sel // 8)]
    ranks = [jnp.zeros(g.shape, F32) for g in groups]
    sub = lax.broadcasted_iota(jnp.int32, groups[0].shape, 0)
    for i in range(n_sel):
        vi = jnp.broadcast_to(val[i:i + 1, :], groups[0].shape)
        for k, vk in enumerate(groups):
            if 8 * k > i:
                ahead = vi >= vk
            elif 8 * k + 7 < i:
                ahead = vi > vk
            else:
                ahead = (vi > vk) | ((vi == vk) & (sub + 8 * k > i))
            ranks[k] = ranks[k] + jnp.where(ahead, 1.0, 0.0)
    rank = jnp.concatenate(ranks, axis=0)
    return (rank < TOP_N) & valid


def _nsa_attn_kernel(qt_ref, ks_ref, kw_ref, vst_ref, vwt_ref, kc_ref, vct_ref, gate_ref, o_ref,
                     selneg_ref, sa_ref, sb_ref, pa_ref, pb_ref, m_ref, acc_ref):
    c = pl.program_id(2)
    nq = Q_BLOCK
    n_sel = ks_ref.shape[0] // SEL_BLOCK
    n_cmp_rows = kc_ref.shape[0]
    ck = SEL_KEY_CHUNK
    wk = WINDOW + nq
    qt = qt_ref[...]
    q_lane = lax.broadcasted_iota(jnp.int32, (1, nq), 1)
    t_lane = c * nq + q_lane

    def heads(x):
        return [x[:, h * nq:(h + 1) * nq] for h in range(NSA_HPG)]

    def add_per_head(x, bias):
        return jnp.concatenate([xh + bias for xh in heads(x)], axis=1)

    def normalised(acc):
        return acc[:NSA_DH] * (1.0 / acc[NSA_DH:NSA_DH + 1])

    w0 = pl.multiple_of(c * nq, nq)
    e_row = lax.broadcasted_iota(jnp.int32, (nq, 1), 0)
    s_c_raw = _dot(kc_ref[...], qt)
    s_w = _dot(kw_ref[pl.ds(w0, wk), :], qt)
    sa_ref[...] = _dot(ks_ref[0:ck, :], qt)

    def window_output():
        s = jnp.concatenate([add_per_head(s_w[:nq], jnp.where(e_row > q_lane, 0.0, NEG_INF)),
                             s_w[nq:WINDOW],
                             add_per_head(s_w[WINDOW:], jnp.where(e_row <= q_lane, 0.0, NEG_INF))], axis=0)
        p = jnp.exp2(s - jnp.max(s, axis=0, keepdims=True)).astype(BF16)
        return normalised(_dot(vwt_ref[:, pl.ds(w0, wk)], p))

    n_row = lax.broadcasted_iota(jnp.int32, (n_cmp_rows, 1), 0)
    ok_c = CMP_STRIDE * n_row + (CMP_LEN - 1) <= t_lane
    any_c = jnp.where(t_lane >= CMP_LEN - 1, 1.0, 0.0)
    s_c = add_per_head(s_c_raw, jnp.where(ok_c, 0.0, NEG_INF))
    e_c = jnp.exp2(s_c - jnp.max(s_c, axis=0, keepdims=True))
    inv_c = jnp.concatenate([any_c] * NSA_HPG, axis=1) / jnp.sum(e_c, axis=0, keepdims=True)
    p_c = e_c * inv_c
    o_cmp = _dot(vct_ref[...], p_c.astype(BF16))
    p_sum = sum(heads(p_c))

    j_row = lax.broadcasted_iota(jnp.int32, (n_sel, 1), 0)
    n_lane = lax.broadcasted_iota(jnp.int32, (1, n_cmp_rows), 1)
    ratio = SEL_BLOCK // CMP_STRIDE
    first_cmp = ratio * j_row - (CMP_LEN - 1) // CMP_STRIDE
    overlap_t = (n_lane >= first_cmp) & (n_lane < ratio * (j_row + 1)) & (n_lane < n_cmp_rows - 1)
    overlap_t = jnp.where(overlap_t, 1.0, 0.0).astype(BF16)
    imp_t = sum(_dot(overlap_t, part) for part in _split3_bf16(p_sum))
    cur = _sel_block_of(t_lane)
    forced = (j_row == 0) | (j_row == cur) | (j_row == cur - 1)
    valid = SEL_BLOCK * j_row <= t_lane
    val = jnp.where(valid, jnp.where(forced, FORCE_SCORE, imp_t), NEG_INF)
    selneg_ref[...] = jnp.where(_top_n_mask(val, valid), 0.0, MASK_OFF)
    o_win = window_output()

    def scores_into(s_ref, k0):
        s_ref[...] = _dot(ks_ref[pl.ds(pl.multiple_of(k0, ck), ck), :], qt)

    def softmax_into(s_ref, p_ref, k0, causal_bias):
        j0 = k0 // SEL_BLOCK
        bias = jnp.concatenate(
            [jnp.broadcast_to(selneg_ref[pl.ds(j0 + b, 1), :], (SEL_BLOCK, nq)) for b in range(ck // SEL_BLOCK)],
            axis=0)
        if causal_bias is not None:
            bias = bias + causal_bias
        s = add_per_head(s_ref[...], bias)
        m_run = m_ref[...]
        m_new = jnp.maximum(m_run, jnp.max(s, axis=0, keepdims=True))
        p_ref[...] = jnp.exp2(s - m_new).astype(p_ref.dtype)
        m_ref[...] = m_new
        return jnp.exp2(m_run - m_new)

    def weighted_values(p_ref, k0):
        return _dot(vst_ref[:, pl.ds(pl.multiple_of(k0, ck), ck)], p_ref[...])

    def fold(alpha, pending):
        acc_ref[...] = alpha * (acc_ref[...] + pending)

    m_ref[...] = jnp.full(m_ref.shape, NEG_INF, F32)
    acc_ref[...] = jnp.zeros(acc_ref.shape, F32)
    pb_ref[...] = jnp.zeros(pb_ref.shape, pb_ref.dtype)
    n_past = (c * nq) // ck

    def two_chunks(i2, carry):
        k0 = pl.multiple_of(2 * i2 * ck, 2 * ck)
        pending = weighted_values(pb_ref, jnp.maximum(k0 - ck, 0))
        scores_into(sb_ref, k0 + ck)
        fold(softmax_into(sa_ref, pa_ref, k0, None), pending)
        pending = weighted_values(pa_ref, k0)
        scores_into(sa_ref, k0 + 2 * ck)
        fold(softmax_into(sb_ref, pb_ref, k0 + ck, None), pending)
        return carry

    lax.fori_loop(0, n_past // 2, two_chunks, 0)
    k_last = pl.multiple_of(n_past * ck, ck)
    key_row = lax.broadcasted_iota(jnp.int32, (ck, 1), 0)
    causal = jnp.where(key_row + k_last <= t_lane, 0.0, NEG_INF)

    @pl.when(n_past % 2 == 1)
    def _():
        pending = weighted_values(pb_ref, jnp.maximum(k_last - 2 * ck, 0))
        scores_into(sb_ref, k_last)
        fold(softmax_into(sa_ref, pa_ref, k_last - ck, None), pending)
        pending = weighted_values(pa_ref, k_last - ck)
        fold(softmax_into(sb_ref, pb_ref, k_last, causal), pending)
        acc_ref[...] = acc_ref[...] + weighted_values(pb_ref, k_last)

    @pl.when(n_past % 2 == 0)
    def _():
        pending = weighted_values(pb_ref, jnp.maximum(k_last - ck, 0))
        fold(softmax_into(sa_ref, pa_ref, k_last, causal), pending)
        acc_ref[...] = acc_ref[...] + weighted_values(pa_ref, k_last)

    o_sel = normalised(acc_ref[...])

    gates_t = jax.nn.sigmoid(gate_ref[...]).T
    outs = [gates_t[3 * h:3 * h + 1] * oc + gates_t[3 * h + 1:3 * h + 2] * osl + gates_t[3 * h + 2:3 * h + 3] * ow
            for h, (oc, osl, ow) in enumerate(zip(heads(o_cmp), heads(o_sel), heads(o_win)))]
    for hp in range(NSA_HPG // 2):
        pair = jnp.concatenate(outs[2 * hp:2 * hp + 2], axis=0)
        o_ref[:, hp * 2 * NSA_DH:(hp + 1) * 2 * NSA_DH] = pair.T.astype(o_ref.dtype)


def _sel_block_of(pos):
    return lax.shift_right_logical(pos, int(math.log2(SEL_BLOCK)))


def _nsa_attn(qt, ks, kw, vst, vwt, kc, vct, zgate, batch):
    t = ks.shape[2]
    m = batch * t
    n_qb = t // Q_BLOCK
    gw = NSA_HPG * NSA_DH
    n_cmp_rows = kc.shape[2]

    def row_map(bb, g, c):
        return (bb * n_qb + c, g)

    def kv_map(bb, g, c):
        return (bb, g, 0, 0)

    def whole(shape):
        return pl.BlockSpec((None, None) + shape, kv_map)

    return pl.pallas_call(
        _nsa_attn_kernel,
        grid=(batch, NSA_KV_GROUPS, n_qb),
        in_specs=[pl.BlockSpec((None, None, AUG_W, NSA_HPG * Q_BLOCK), lambda bb, g, c: (bb, g, 0, c)),
                  whole((t, AUG_W)), whole((WINDOW + t, AUG_W)),
                  whole((V_ROWS, t)), whole((V_ROWS, WINDOW + t)),
                  whole((n_cmp_rows, AUG_W)), whole((NSA_DH, n_cmp_rows)),
                  pl.BlockSpec((Q_BLOCK, 128), row_map)],
        out_specs=pl.BlockSpec((Q_BLOCK, gw), row_map),
        out_shape=jax.ShapeDtypeStruct((m, NSA_KV_GROUPS * gw), BF16),
        scratch_shapes=[pltpu.VMEM((t // SEL_BLOCK, Q_BLOCK), F32),
                        pltpu.VMEM((SEL_KEY_CHUNK, NSA_HPG * Q_BLOCK), F32),
                        pltpu.VMEM((SEL_KEY_CHUNK, NSA_HPG * Q_BLOCK), F32),
                        pltpu.VMEM((SEL_KEY_CHUNK, NSA_HPG * Q_BLOCK), BF16),
                        pltpu.VMEM((SEL_KEY_CHUNK, NSA_HPG * Q_BLOCK), BF16),
                        pltpu.VMEM((1, NSA_HPG * Q_BLOCK), F32),
                        pltpu.VMEM((V_ROWS, NSA_HPG * Q_BLOCK), F32)],
        compiler_params=_cparams("parallel", "parallel", "arbitrary"),
        name="nsa_attn",
    )(qt, ks, kw, vst, vwt, kc, vct, zgate)


def _nsa_q_extras():
    slopes = jnp.asarray(2.0 ** (-8.0 * np.arange(1, NSA_HEADS + 1) / NSA_HEADS) * LOG2E, dtype=F32)
    parts = jnp.stack([p.astype(F32) for p in _split3_bf16(slopes)], axis=1)
    ex = jnp.zeros((NSA_HEADS, AUG_W), F32)
    ex = ex.at[:, AUG_A:AUG_A + 3].set(parts * SEL_BLOCK)
    ex = ex.at[:, AUG_B:AUG_B + 3].set(parts)
    ex = ex.at[:, AUG_N:AUG_N + 3].set(parts * CMP_STRIDE)
    return ex.at[:, AUG_FLAG].set(1.0)


def _merge_kernel(h_ref, a_ref, b_ref, c_ref, zm_ref, wa_ref, wb_ref, wc_ref, wo_ref, o_ref):
    d = h_ref.shape[1]
    gates = jax.nn.sigmoid(zm_ref[...].astype(F32))
    y = (gates[:, :d] * _dot(a_ref[...], wa_ref[...])
         + gates[:, d:2 * d] * _dot(b_ref[...], wb_ref[...])
         + gates[:, 2 * d:] * _dot(c_ref[...], wc_ref[...]))
    o_ref[...] = h_ref[...] + _dot(y.astype(BF16), wo_ref[...])


def _merge(h, a, b, c, zm, wa, wb, wc, wo, l):
    m, d = h.shape
    tm = ROW_TILE

    def rows(n):
        return pl.BlockSpec((tm, n), lambda i: (i, 0))

    return pl.pallas_call(
        _merge_kernel,
        grid=(m // tm,),
        in_specs=[rows(d), rows(a.shape[1]), rows(b.shape[1]), rows(c.shape[1]), rows(zm.shape[1]),
                  _layer(wa.shape, l), _layer(wb.shape, l), _layer(wc.shape, l), _layer(wo.shape, l)],
        out_specs=rows(d),
        out_shape=jax.ShapeDtypeStruct((m, d), F32),
        compiler_params=_cparams("parallel"),
        name="merge",
    )(h, a, b, c, zm, wa, wb, wc, wo)


def _xkv_kernel(mem_ref, g_ref, w_ref, kg_ref, k_ref, v_ref):
    mn = _rms(mem_ref[...], g_ref[...]).astype(BF16)
    kv = _dot(mn, w_ref[...])
    dk = k_ref.shape[1]
    for h in range(X_HEADS):
        sl = slice(h * X_DH, (h + 1) * X_DH)
        k_ref[:, sl] = _rms(kv[:, sl], kg_ref[...]).astype(k_ref.dtype)
    v_ref[...] = kv[:, dk:].astype(v_ref.dtype)


def _xkv(mem2, g, w, kg, l, batch):
    rows, d = mem2.shape
    mlen = rows // batch
    dk = X_HEADS * X_DH
    return pl.pallas_call(
        _xkv_kernel,
        grid=(batch,),
        in_specs=[pl.BlockSpec((mlen, d), lambda bb: (bb, 0)),
                  _resident((1, d)), _layer(w.shape, l), _resident((1, X_DH))],
        out_specs=[pl.BlockSpec((mlen, dk), lambda bb: (bb, 0)), pl.BlockSpec((mlen, dk), lambda bb: (bb, 0))],
        out_shape=[jax.ShapeDtypeStruct((rows, dk), BF16), jax.ShapeDtypeStruct((rows, dk), BF16)],
        compiler_params=_cparams("parallel"),
        name="xattn_kv",
    )(mem2, g.reshape(1, d), w, kg.reshape(1, X_DH))


def _xattn_kernel(h_ref, g_ref, wq_ref, qg_ref, k_ref, v_ref, wo_ref, o_ref):
    h = h_ref[...]
    q = _dot(_rms(h, g_ref[...]).astype(BF16), wq_ref[...])
    scale = 1.0 / math.sqrt(X_DH)
    outs = []
    for hd in range(X_HEADS):
        sl = slice(hd * X_DH, (hd + 1) * X_DH)
        qh = _rms(q[:, sl], qg_ref[...]).astype(BF16)
        s = _dot_nt(qh, k_ref[:, sl]) * scale
        e = jnp.exp(s - jnp.max(s, axis=-1, keepdims=True))
        p = e / jnp.sum(e, axis=-1, keepdims=True)
        outs.append(_dot(p.astype(BF16), v_ref[:, sl]))
    o = jnp.concatenate(outs, axis=-1).astype(BF16)
    o_ref[...] = h + _dot(o, wo_ref[...])


def _xattn(h, g, wq, qg, k, v, wo, l, batch):
    m, d = h.shape
    tm = ROW_TILE
    per_b = m // batch // tm
    mlen = k.shape[0] // batch
    dk = k.shape[1]
    return pl.pallas_call(
        _xattn_kernel,
        grid=(batch, per_b),
        in_specs=[pl.BlockSpec((tm, d), lambda bb, i: (bb * per_b + i, 0)),
                  _resident((1, d)), _layer(wq.shape, l), _resident((1, X_DH)),
                  pl.BlockSpec((mlen, dk), lambda bb, i: (bb, 0)),
                  pl.BlockSpec((mlen, dk), lambda bb, i: (bb, 0)),
                  _layer(wo.shape, l)],
        out_specs=pl.BlockSpec((tm, d), lambda bb, i: (bb * per_b + i, 0)),
        out_shape=jax.ShapeDtypeStruct((m, d), F32),
        compiler_params=_cparams("parallel", "parallel"),
        name="xattn",
    )(h, g.reshape(1, d), wq, qg.reshape(1, X_DH), k, v, wo)


def _inproj_layout(d_conv, d_sgu):
    cols_q = NSA_HEADS * NSA_DH
    kv0 = 2 * d_conv + 2 * d_sgu + cols_q
    gdh = NSA_KV_GROUPS * NSA_DH
    gate0 = kv0 + 6 * gdh
    merge0 = gate0 + NSA_HEADS * 3
    d = np.arange(NSA_DH)
    kvs = np.concatenate([kv0 + i * gdh + g * NSA_DH + d
                          for g in range(NSA_KV_GROUPS) for i in (2, 4, 3, 5)])
    cmp = kv0 + np.arange(2 * gdh)
    return kv0, kvs, cmp, gate0, merge0


def _prep_w_in(w_in, d_conv, d_sgu):
    kv0, kvs, cmp, gate0, merge0 = _inproj_layout(d_conv, d_sgu)
    w_in = w_in.astype(BF16)
    per_g = NSA_HPG * 3
    gate_blocks = []
    for g in range(NSA_KV_GROUPS):
        blk = w_in[:, :, gate0 + g * per_g:gate0 + (g + 1) * per_g]
        gate_blocks.append(jnp.pad(blk, ((0, 0), (0, 0), (0, 128 - per_g))))
    w_all = jnp.concatenate([w_in[:, :, :kv0], w_in[:, :, kvs], w_in[:, :, cmp]] + gate_blocks
                            + [w_in[:, :, merge0:]], axis=-1)
    d_model = w_in.shape[1]
    widths = (2 * d_conv, 2 * d_sgu, NSA_HEADS * NSA_DH, len(kvs), len(cmp), 128 * NSA_KV_GROUPS, 3 * d_model)
    dtypes = (F32,) * (len(widths) - 1) + (BF16,)
    return w_all, widths, dtypes


def kernel(x, mem, norm_ffn1, ffn1_wi, ffn1_wo, norm_mix, w_in, conv_w, conv_b, conv_ln_g, conv_ln_b, conv_out,
           sgu_norm, sgu_ws, sgu_b, sgu_out, nsa_q_norm, nsa_k_norm, cmp_pe, cmp_w1, cmp_w2, nsa_out, w_out,
           norm_xattn, mem_norm, xq, xkv, xq_norm, xk_norm, xo, norm_ffn2, ffn2_wi, ffn2_wo):
    batch, t, d_model = x.shape
    depth = w_in.shape[0]
    d_conv = conv_w.shape[-1]
    d_sgu = sgu_norm.shape[-1]
    m = batch * t

    w_all, widths, dtypes = _prep_w_in(w_in, d_conv, d_sgu)
    bf = lambda w: w.astype(BF16)
    ffn1_wi, ffn1_wo, ffn2_wi, ffn2_wo = bf(ffn1_wi), bf(ffn1_wo), bf(ffn2_wi), bf(ffn2_wo)
    conv_out, sgu_out, nsa_out, w_out = bf(conv_out), bf(sgu_out), bf(nsa_out), bf(w_out)
    xq, xkv, xo = bf(xq), bf(xkv), bf(xo)
    sgu_b_exp = jnp.repeat(jnp.swapaxes(sgu_b, 1, 2), d_sgu // SGU_GROUPS, axis=2)
    q_extras = _nsa_q_extras()
    pe_rows = jnp.tile(cmp_pe, (1, 1, 1, NSA_KV_GROUPS))
    w1_bd = _group_block_diag(cmp_w1.reshape(depth, 2, CMP_LEN, NSA_DH, NSA_DH)).astype(BF16)
    w2_bd = _group_block_diag(cmp_w2).astype(BF16)
    kg0_2 = jnp.tile(nsa_k_norm[:, 0], (1, 2)).reshape(depth, 1, 2 * NSA_DH)

    h = x.reshape(m, d_model)
    mem2 = mem.reshape(-1, d_model)
    for l in range(depth):
        h = _ffn(h, norm_ffn1[l], ffn1_wi, ffn1_wo, l)

        z_conv, z_sgu, z_q, z_kvs, z_cmp, z_gate, z_merge = _inproj(h, norm_mix[l], w_all, l, widths, dtypes)
        act_a = _conv(z_conv, conv_w[l], conv_b[l], conv_ln_g[l], conv_ln_b[l], batch)
        act_b = _sgu(z_sgu, sgu_norm[l], sgu_ws[l], sgu_b_exp[l])

        qg2 = jnp.tile(nsa_q_norm[l], 2).reshape(1, 2 * NSA_DH)
        kg2 = jnp.concatenate([nsa_k_norm[l, 1], nsa_k_norm[l, 2]]).reshape(1, 2 * NSA_DH)
        qt, ks, kw, vst, vwt = _nsa_prep(z_q, z_kvs, qg2, kg2, q_extras, batch)
        kc, vct = _nsa_cmp(z_cmp, pe_rows[l], w1_bd, w2_bd, kg0_2[l], l, batch)
        act_c = _nsa_attn(qt, ks, kw, vst, vwt, kc, vct, z_gate, batch)

        h = _merge(h, act_a, act_b, act_c, z_merge, conv_out, sgu_out, nsa_out, w_out, l)

        xk, xv = _xkv(mem2, mem_norm[l], xkv, xk_norm[l], l, batch)
        h = _xattn(h, norm_xattn[l], xq, xq_norm[l], xk, xv, xo, l, batch)

        h = _ffn(h, norm_ffn2[l], ffn2_wi, ffn2_wo, l)
    return h.reshape(batch, t, d_model)
```
